```python
import math
import jax, jax.numpy as jnp
from jax import lax
import numpy as np

D_MODEL = 4096
BATCH = 1
SEQ = 8192
DEPTH = 2
DEC_BATCH = 4
DEC_SEQ = 4096
PAST_LEN = 128

GRID_W = 64
HEAD_DIM = 128
NA_HEADS = 8
NA_WIN_R = 8
NA_WIN_C = 16
NA_QBLK_C = 16
NA_SPAN_C = 32
DIFF_HEADS = 12
DIFF_QK_DIM = 64
DIFF_V_DIM = 128
SWA_Q_HEADS = 12
SWA_KV_HEADS = 4
SWA_WINDOW = 128
BLOCK = 128
REL_BUCKETS = 32
REL_MAX_DIST = 128
D_FF = 11008
PLE_DIM = 256
LN_EPS = 1e-5
RMS_EPS = 1e-5
NEG_INF = -1e30
DEEPNORM_ALPHA = (2 * DEPTH) ** 0.25
DEEPNORM_BETA = (8 * DEPTH) ** -0.25

A_W = NA_HEADS * HEAD_DIM
B_QK_W = DIFF_HEADS * 2 * DIFF_QK_DIM
B_V_W = DIFF_HEADS * DIFF_V_DIM
C_Q_W = SWA_Q_HEADS * HEAD_DIM
C_KV_W = SWA_KV_HEADS * HEAD_DIM
MIX_W = A_W + B_V_W + C_Q_W
SEG_WIDTHS = (A_W, A_W, A_W, B_QK_W, B_QK_W, B_V_W, C_Q_W, C_KV_W, C_KV_W)
V_SEGMENTS = (2, 5, 8)
IN_W = sum(SEG_WIDTHS)
SPLIT_POINTS = tuple(sum(SEG_WIDTHS[:i + 1]) for i in range(len(SEG_WIDTHS) - 1))

kernel_name = 'hymba_style_bidir_na_diff_swa_encoder'


def layer_norm(x, g, b):
    xf = x.astype(jnp.float32)
    mu = jnp.mean(xf, axis=-1, keepdims=True)
    var = jnp.mean(jnp.square(xf - mu), axis=-1, keepdims=True)
    y = (xf - mu) * lax.rsqrt(var + LN_EPS) * g.astype(jnp.float32) + b.astype(jnp.float32)
    return y.astype(x.dtype)


def swiglu(x, w_gate, w_up, w_down):
    return (jax.nn.silu(x @ w_gate) * (x @ w_up)) @ w_down


def t5_bucket(rel):
    half = REL_BUCKETS // 2
    max_exact = half // 2
    n = jnp.abs(rel)
    side = jnp.where(rel > 0, half, 0)
    nf = jnp.maximum(n, 1).astype(jnp.float32)
    large = max_exact + (jnp.log(nf / max_exact) / math.log(REL_MAX_DIST / max_exact)
                         * (half - max_exact)).astype(jnp.int32)
    large = jnp.minimum(large, half - 1)
    return side + jnp.where(n < max_exact, n, large)


def neighbourhood_attention(q, k, v, rpb):
    bsz, t, h, d = q.shape
    rows = t // GRID_W
    kr = min(NA_WIN_R, rows)
    qr = 8 if rows % 8 == 0 else (4 if rows % 4 == 0 else 2)
    sr = min(qr + kr - 1, rows)
    n_rb, n_cb = rows // qr, GRID_W // NA_QBLK_C
    qc, sc, kc = NA_QBLK_C, NA_SPAN_C, NA_WIN_C
    win_r0 = np.clip(np.arange(rows) - kr // 2, 0, rows - kr)
    win_c0 = np.clip(np.arange(GRID_W) - kc // 2, 0, GRID_W - kc)
    q_rows = np.arange(n_rb)[:, None] * qr + np.arange(qr)[None, :]
    q_cols = np.arange(n_cb)[:, None] * qc + np.arange(qc)[None, :]
    k_rows = np.clip(np.arange(n_rb) * qr - kr // 2, 0, rows - sr)[:, None] + np.arange(sr)[None, :]
    k_cols = np.clip(np.arange(n_cb) * qc - kc // 2, 0, GRID_W - sc)[:, None] + np.arange(sc)[None, :]
    r0 = win_r0[q_rows][:, :, None]
    c0 = win_c0[q_cols][:, :, None]
    row_ok = (k_rows[:, None, :] >= r0) & (k_rows[:, None, :] < r0 + kr)
    col_ok = (k_cols[:, None, :] >= c0) & (k_cols[:, None, :] < c0 + kc)
    mask = row_ok[:, None, None, :, None, :, None] & col_ok[None, :, None, None, :, None, :]
    ri = np.clip(k_rows[:, None, :] - q_rows[:, :, None] + NA_WIN_R - 1, 0, 2 * NA_WIN_R - 2).astype(np.int32)
    ci = np.clip(k_cols[:, None, :] - q_cols[:, :, None] + NA_WIN_C - 1, 0, 2 * NA_WIN_C - 2).astype(np.int32)
    bias = rpb[:, ri[:, None, :, None, :, None], ci[None, :, None, :, None, :]]
    bias = jnp.moveaxis(bias, 0, 2).astype(jnp.float32)
    kri = k_rows.astype(np.int32)[:, None, :, None]
    kci = k_cols.astype(np.int32)[None, :, None, :]
    kg = k.reshape(bsz, rows, GRID_W, h, d)[:, kri, kci]
    vg = v.reshape(bsz, rows, GRID_W, h, d)[:, kri, kci]
    qg = q.reshape(bsz, n_rb, qr, n_cb, qc, h, d)
    s = jnp.einsum('bnimjhd,bnmsthd->bnmhijst', qg, kg).astype(jnp.float32) * (d ** -0.5)
    s = jnp.where(mask, s + bias, NEG_INF)
    pr = jax.nn.softmax(s, axis=(-2, -1))
    o = jnp.einsum('bnmhijst,bnmsthd->bnimjhd', pr.astype(v.dtype), vg)
    return o.reshape(bsz, t, h * d)


def differential_attention(q, k, v, rel_table, lam, lambda_init, sub_g):
    bsz, t, h, _, dk = q.shape
    dv = v.shape[-1]
    nb = t // BLOCK
    k_pos = jnp.arange(t, dtype=jnp.int32)
    qb = jnp.swapaxes(q.reshape(bsz, nb, BLOCK, h, 2, dk), 0, 1)

    def one_block(args):
        qblk, i = args
        q_pos = i * BLOCK + jnp.arange(BLOCK, dtype=jnp.int32)
        bias = rel_table[t5_bucket(k_pos[None, :] - q_pos[:, None])]
        bias = jnp.moveaxis(bias, -1, 0).astype(jnp.float32)
        s = jnp.einsum('bqhcd,bkhcd->bhcqk', qblk, k).astype(jnp.float32) * (dk ** -0.5)
        pr = jax.nn.softmax(s + bias[None, :, None], axis=-1)
        a = pr[:, :, 0] - lam * pr[:, :, 1]
        return jnp.einsum('bhqk,bkhe->bqhe', a.astype(v.dtype), v)

    o = lax.map(one_block, (qb, jnp.arange(nb, dtype=jnp.int32)))
    o = jnp.swapaxes(o, 0, 1).reshape(bsz, t, h, dv).astype(jnp.float32)
    o = o * lax.rsqrt(jnp.mean(jnp.square(o), axis=-1, keepdims=True) + RMS_EPS)
    o = o * sub_g.astype(jnp.float32) * (1.0 - lambda_init)
    return o.reshape(bsz, t, h * dv).astype(v.dtype)


def sliding_window_gqa(q, k, v, rel_table, sink):
    bsz, t, hq, d = q.shape
    hkv = k.shape[2]
    g = hq // hkv
    nb = t // BLOCK
    pad = ((0, 0), (BLOCK, BLOCK), (0, 0), (0, 0))
    kp = jnp.pad(k, pad).reshape(bsz, nb + 2, BLOCK, hkv, d)
    vp = jnp.pad(v, pad).reshape(bsz, nb + 2, BLOCK, hkv, d)
    kw = jnp.concatenate([kp[:, :-2], kp[:, 1:-1], kp[:, 2:]], axis=2)
    vw = jnp.concatenate([vp[:, :-2], vp[:, 1:-1], vp[:, 2:]], axis=2)
    qb = q.reshape(bsz, nb, BLOCK, hkv, g, d)
    s = jnp.einsum('bnqkgd,bnckd->bnkgqc', qb, kw).astype(jnp.float32) * (d ** -0.5)
    rel = (np.arange(3 * BLOCK)[None, :] - BLOCK) - np.arange(BLOCK)[:, None]
    bias = rel_table[t5_bucket(jnp.asarray(rel, dtype=jnp.int32))]
    bias = jnp.transpose(bias, (2, 0, 1)).reshape(hkv, g, BLOCK, 3 * BLOCK).astype(jnp.float32)
    key_pos = (np.arange(nb)[:, None] - 1) * BLOCK + np.arange(3 * BLOCK)[None, :]
    mask = (np.abs(rel) <= SWA_WINDOW)[None] & ((key_pos >= 0) & (key_pos < t))[:, None, :]
    s = jnp.where(mask[None, :, None, None], s + bias, NEG_INF)
    sink_l = jnp.broadcast_to(sink.reshape(hkv, g, 1, 1).astype(jnp.float32), s.shape[:-1] + (1,))
    pr = jax.nn.softmax(jnp.concatenate([s, sink_l], axis=-1), axis=-1)[..., :-1]
    o = jnp.einsum('bnkgqc,bnckd->bnqkgd', pr.astype(v.dtype), vw)
    return o.reshape(bsz, t, hq * d)


def encoder_layer(x, p, li, w_in, w_out, na_rpb, lam_q1, lam_k1, lam_q2, lam_k2,
                  diff_g, sink, rel_bias, f1_gate, f1_up, f1_down,
                  f2_gate, f2_up, f2_down, ple_gate, ple_proj, ln_g, ln_b):
    bsz, t, _ = x.shape
    x = layer_norm(DEEPNORM_ALPHA * x + 0.5 * swiglu(x, f1_gate, f1_up, f1_down), ln_g[0], ln_b[0])
    qa, ka, va, qb, kb, vb, qc, kc, vc = jnp.split(x @ w_in, SPLIT_POINTS, axis=-1)
    a_out = neighbourhood_attention(qa.reshape(bsz, t, NA_HEADS, HEAD_DIM),
                                    ka.reshape(bsz, t, NA_HEADS, HEAD_DIM),
                                    va.reshape(bsz, t, NA_HEADS, HEAD_DIM), na_rpb)
    lambda_init = 0.8 - 0.6 * math.exp(-0.3 * li)
    lam = (jnp.exp(jnp.sum(lam_q1.astype(jnp.float32) * lam_k1.astype(jnp.float32)))
           - jnp.exp(jnp.sum(lam_q2.astype(jnp.float32) * lam_k2.astype(jnp.float32))) + lambda_init)
    b_out = differential_attention(qb.reshape(bsz, t, DIFF_HEADS, 2, DIFF_QK_DIM),
                                   kb.reshape(bsz, t, DIFF_HEADS, 2, DIFF_QK_DIM),
                                   vb.reshape(bsz, t, DIFF_HEADS, DIFF_V_DIM),
                                   rel_bias[:, :DIFF_HEADS], lam, lambda_init, diff_g)
    c_out = sliding_window_gqa(qc.reshape(bsz, t, SWA_Q_HEADS, HEAD_DIM),
                               kc.reshape(bsz, t, SWA_KV_HEADS, HEAD_DIM),
                               vc.reshape(bsz, t, SWA_KV_HEADS, HEAD_DIM),
                               rel_bias[:, DIFF_HEADS:], sink)
    mix = jnp.concatenate([a_out, b_out, c_out], axis=-1) @ w_out
    x = layer_norm(DEEPNORM_ALPHA * x + mix, ln_g[1], ln_b[1])
    ple = jax.nn.sigmoid(x @ ple_gate) * (p @ ple_proj)
    x = layer_norm(DEEPNORM_ALPHA * x + 0.5 * swiglu(x, f2_gate, f2_up, f2_down) + ple, ln_g[2], ln_b[2])
    return x


def setup_inputs(seed: int = 0) -> dict:
    key = jax.random.key(seed)
    ks = jax.random.split(key, 24)
    f32 = jnp.float32

    def nrm(k, shape, scale):
        return jax.random.normal(k, shape, f32) * scale

    col_scale = jnp.concatenate([jnp.full((w,), DEEPNORM_BETA if i in V_SEGMENTS else 1.0, f32)
                                 for i, w in enumerate(SEG_WIDTHS)])
    return {
        'x_prompt': nrm(ks[0], (BATCH, SEQ, D_MODEL), 1.0),
        'x_sample': nrm(ks[1], (DEC_BATCH, DEC_SEQ, D_MODEL), 1.0),
        'p_prompt': nrm(ks[2], (DEPTH, BATCH, SEQ, PLE_DIM), 1.0),
        'p_sample': nrm(ks[3], (DEPTH, DEC_BATCH, DEC_SEQ, PLE_DIM), 1.0),
        'w_in': nrm(ks[4], (DEPTH, D_MODEL, IN_W), D_MODEL ** -0.5) * col_scale,
        'w_out': nrm(ks[5], (DEPTH, MIX_W, D_MODEL), DEEPNORM_BETA * MIX_W ** -0.5),
        'na_rpb': nrm(ks[6], (DEPTH, NA_HEADS, 2 * NA_WIN_R - 1, 2 * NA_WIN_C - 1), 0.1),
        'lam_q1': nrm(ks[7], (DEPTH, DIFF_QK_DIM), 0.1),
        'lam_k1': nrm(ks[8], (DEPTH, DIFF_QK_DIM), 0.1),
        'lam_q2': nrm(ks[9], (DEPTH, DIFF_QK_DIM), 0.1),
        'lam_k2': nrm(ks[10], (DEPTH, DIFF_QK_DIM), 0.1),
        'diff_g': 1.0 + nrm(ks[11], (DEPTH, DIFF_V_DIM), 0.02),
        'sink': nrm(ks[12], (DEPTH, SWA_Q_HEADS), 0.5),
        'rel_bias': nrm(ks[13], (REL_BUCKETS, DIFF_HEADS + SWA_Q_HEADS), 0.1),
        'ffn1_gate': nrm(ks[14], (DEPTH, D_MODEL, D_FF), D_MODEL ** -0.5),
        'ffn1_up': nrm(ks[15], (DEPTH, D_MODEL, D_FF), D_MODEL ** -0.5),
        'ffn1_down': nrm(ks[16], (DEPTH, D_FF, D_MODEL), DEEPNORM_BETA * D_FF ** -0.5),
        'ffn2_gate': nrm(ks[17], (DEPTH, D_MODEL, D_FF), D_MODEL ** -0.5),
        'ffn2_up': nrm(ks[18], (DEPTH, D_MODEL, D_FF), D_MODEL ** -0.5),
        'ffn2_down': nrm(ks[19], (DEPTH, D_FF, D_MODEL), DEEPNORM_BETA * D_FF ** -0.5),
        'ple_gate': nrm(ks[20], (DEPTH, D_MODEL, D_MODEL), D_MODEL ** -0.5),
        'ple_proj': nrm(ks[21], (DEPTH, PLE_DIM, D_MODEL), DEEPNORM_BETA * PLE_DIM ** -0.5),
        'ln_g': 1.0 + nrm(ks[22], (DEPTH, 3, D_MODEL), 0.02),
        'ln_b': nrm(ks[23], (DEPTH, 3, D_MODEL), 0.02),
    }


def reference(x_prompt, x_sample, p_prompt, p_sample, w_in, w_out, na_rpb,
              lam_q1, lam_k1, lam_q2, lam_k2, diff_g, sink, rel_bias,
              ffn1_gate, ffn1_up, ffn1_down, ffn2_gate, ffn2_up, ffn2_down,
              ple_gate, ple_proj, ln_g, ln_b):
    def trunk(x, p):
        for li in range(DEPTH):
            x = encoder_layer(x, p[li], li, w_in[li], w_out[li], na_rpb[li],
                              lam_q1[li], lam_k1[li], lam_q2[li], lam_k2[li],
                              diff_g[li], sink[li], rel_bias,
                              ffn1_gate[li], ffn1_up[li], ffn1_down[li],
                              ffn2_gate[li], ffn2_up[li], ffn2_down[li],
                              ple_gate[li], ple_proj[li], ln_g[li], ln_b[li])
        return x

    y_prompt = trunk(x_prompt, p_prompt)
    y_sample = trunk(x_sample, p_sample)
    return (y_prompt, y_sample)
```

```python
import functools
import math

import jax
import jax.numpy as jnp
import numpy as np
from jax import lax
from jax.experimental import pallas as pl
from jax.experimental.pallas import tpu as pltpu

F32 = jnp.float32
BF16 = jnp.bfloat16

D_MODEL = 4096
DEPTH = 2
GRID_W = 64
HEAD_DIM = 128
NA_HEADS = 8
NA_WIN_R = 8
NA_WIN_C = 16
DIFF_HEADS = 12
DIFF_QK_DIM = 64
SWA_Q_HEADS = 12
SWA_KV_HEADS = 4
SWA_GROUP = SWA_Q_HEADS // SWA_KV_HEADS
SWA_WINDOW = 128
REL_BUCKETS = 32
REL_MAX_DIST = 128
D_FF = 11008
PLE_DIM = 256
LN_EPS = 1e-5
RMS_EPS = 1e-5
NEG_INF = -1e30
DEEPNORM_ALPHA = (2 * DEPTH) ** 0.25

A_W = NA_HEADS * HEAD_DIM
B_W = DIFF_HEADS * HEAD_DIM
C_Q_W = SWA_Q_HEADS * HEAD_DIM
C_KV_W = SWA_KV_HEADS * HEAD_DIM
IN_W = 3 * A_W + 3 * B_W + C_Q_W + 2 * C_KV_W
COL_QA, COL_KA, COL_VA = 0, A_W // 128, 2 * A_W // 128
COL_QB = 3 * A_W // 128
COL_KB = COL_QB + B_W // 128
COL_VB = COL_KB + B_W // 128
COL_QC = COL_VB + B_W // 128
COL_KC = COL_QC + C_Q_W // 128
COL_VC = COL_KC + C_KV_W // 128

V7X_VMEM_BYTES = 64 * 1024 * 1024
VMEM_LIMIT = 56 * 1024 * 1024

FF_TILE = 512
D_FF_PAD = -(-D_FF // 1024) * 1024

NT_DIMS = (((1,), (1,)), ((), ()))
TN_DIMS = (((0,), (0,)), ((), ()))


def _cparams(sem):
    return pltpu.CompilerParams(dimension_semantics=sem, vmem_limit_bytes=VMEM_LIMIT)


def _gateup_kernel(x_ref, wg_ref, wu_ref, o_ref):
    x = x_ref[...]
    g = jnp.dot(x, wg_ref[...], preferred_element_type=F32)
    u = jnp.dot(x, wu_ref[...], preferred_element_type=F32)
    o_ref[...] = (g * jax.nn.sigmoid(g) * u).astype(o_ref.dtype)


def gateup(x, wg, wu):
    m, k = x.shape
    f = wg.shape[1]
    tm = min(1024, m)
    tn = min(FF_TILE, f)
    return pl.pallas_call(
        _gateup_kernel,
        grid=(m // tm, f // tn),
        in_specs=[
            pl.BlockSpec((tm, k), lambda i, j: (i, 0)),
            pl.BlockSpec((k, tn), lambda i, j: (0, j)),
            pl.BlockSpec((k, tn), lambda i, j: (0, j)),
        ],
        out_specs=pl.BlockSpec((tm, tn), lambda i, j: (i, j)),
        out_shape=jax.ShapeDtypeStruct((m, f), BF16),
        compiler_params=_cparams(("parallel", "arbitrary")),
        name="gateup",
    )(x, wg, wu)


def _matmul_kernel(a_ref, w_ref, o_ref, acc_ref, *, nk):
    kk = pl.program_id(2)
    part = jnp.dot(a_ref[...], w_ref[...], preferred_element_type=F32)
    if nk == 1:
        o_ref[...] = part.astype(o_ref.dtype)
        return

    @pl.when(kk == 0)
    def _():
        acc_ref[...] = part

    @pl.when(jnp.logical_and(kk > 0, kk < nk - 1))
    def _():
        acc_ref[...] += part

    @pl.when(kk == nk - 1)
    def _():
        o_ref[...] = (acc_ref[...] + part).astype(o_ref.dtype)


def matmul(a, w, out_dtype, tk=None):
    m, k = a.shape
    n = w.shape[1]
    tm = min(1024, m)
    tn = min(1024, n)
    tk = k if tk is None else tk
    nk = k // tk
    return pl.pallas_call(
        functools.partial(_matmul_kernel, nk=nk),
        grid=(m // tm, n // tn, nk),
        in_specs=[
            pl.BlockSpec((tm, tk), lambda i, j, kk: (i, kk)),
            pl.BlockSpec((tk, tn), lambda i, j, kk: (kk, j)),
        ],
        out_specs=pl.BlockSpec((tm, tn), lambda i, j, kk: (i, j)),
        out_shape=jax.ShapeDtypeStruct((m, n), out_dtype),
        scratch_shapes=[pltpu.VMEM((tm, tn) if nk > 1 else (8, 128), F32)],
        compiler_params=_cparams(("parallel", "parallel", "arbitrary")),
        name="matmul",
    )(a, w)


def _ple_kernel(x_ref, xb_ref, p_ref, wg_ref, wp_ref, o_ref):
    gate = jax.nn.sigmoid(jnp.dot(xb_ref[...], wg_ref[...], preferred_element_type=F32))
    proj = jnp.dot(p_ref[...], wp_ref[...], preferred_element_type=F32)
    o_ref[...] = DEEPNORM_ALPHA * x_ref[...] + gate * proj


def ple_residual(x, xb, p, wg, wp):
    m, d = x.shape
    tm = min(512, m)
    tn = min(1024, d)
    return pl.pallas_call(
        _ple_kernel,
        grid=(m // tm, d // tn),
        in_specs=[
            pl.BlockSpec((tm, tn), lambda i, j: (i, j)),
            pl.BlockSpec((tm, d), lambda i, j: (i, 0)),
            pl.BlockSpec((tm, p.shape[1]), lambda i, j: (i, 0)),
            pl.BlockSpec((d, tn), lambda i, j: (0, j)),
            pl.BlockSpec((p.shape[1], tn), lambda i, j: (0, j)),
        ],
        out_specs=pl.BlockSpec((tm, tn), lambda i, j: (i, j)),
        out_shape=jax.ShapeDtypeStruct((m, d), F32),
        compiler_params=_cparams(("parallel", "arbitrary")),
        name="ple_residual",
    )(x, xb, p, wg, wp)


def _ln_kernel(res_ref, r_ref, g_ref, b_ref, of_ref, ob_ref, *, res_scale, r_scale):
    y = res_scale * res_ref[...] + r_scale * r_ref[...]
    mu = jnp.mean(y, axis=-1, keepdims=True)
    yc = y - mu
    var = jnp.mean(yc * yc, axis=-1, keepdims=True)
    out = yc * lax.rsqrt(var + LN_EPS) * g_ref[...] + b_ref[...]
    of_ref[...] = out
    ob_ref[...] = out.astype(BF16)


def residual_layernorm(res, r, g, b, res_scale, r_scale):
    m, d = res.shape
    tm = min(256, m)
    row = pl.BlockSpec((tm, d), lambda i: (i, 0))
    vec = pl.BlockSpec((1, d), lambda i: (0, 0))
    return pl.pallas_call(
        functools.partial(_ln_kernel, res_scale=res_scale, r_scale=r_scale),
        grid=(m // tm,),
        in_specs=[row, row, vec, vec],
        out_specs=[row, row],
        out_shape=[jax.ShapeDtypeStruct((m, d), F32), jax.ShapeDtypeStruct((m, d), BF16)],
        compiler_params=_cparams(("parallel",)),
        name="residual_layernorm",
    )(res, r, g.reshape(1, d), b.reshape(1, d))


def _t5_bucket(rel):
    half = REL_BUCKETS // 2
    max_exact = half // 2
    n = jnp.abs(rel)
    side = jnp.where(rel > 0, half, 0)
    nf = jnp.maximum(n, 1).astype(F32)
    large = max_exact + (jnp.log(nf / max_exact) / math.log(REL_MAX_DIST / max_exact)
                         * (half - max_exact)).astype(jnp.int32)
    large = jnp.minimum(large, half - 1)
    return side + jnp.where(n < max_exact, n, large)


DIFF_TILE = 256
SWA_TQ = 256


def diff_bias_tables(rel_table):
    t = DIFF_TILE
    kq = np.arange(t)[:, None] - np.arange(t)[None, :]
    rel = np.stack([kq + o * t for o in (-1, 0, 1)]).astype(np.int32)
    tiles = rel_table[_t5_bucket(jnp.asarray(rel))]
    far = rel_table[_t5_bucket(jnp.asarray([-2 * t, 2 * t], dtype=jnp.int32))]
    return jnp.moveaxis(tiles, -1, 0).astype(F32), far.astype(F32)


def swa_bias_table(rel_table):
    rel = (np.arange(SWA_TQ + 2 * SWA_WINDOW)[None, :] - SWA_WINDOW) - np.arange(SWA_TQ)[:, None]
    bias = rel_table[_t5_bucket(jnp.asarray(rel, dtype=jnp.int32))]
    bias = jnp.moveaxis(bias, -1, 0).astype(F32)
    return jnp.where(jnp.asarray(np.abs(rel) <= SWA_WINDOW)[None], bias, NEG_INF)


def na_bias_table(rpb):
    qc = np.arange(GRID_W)[:, None]
    kc = np.arange(GRID_W)[None, :]
    c0 = np.clip(qc - NA_WIN_C // 2, 0, GRID_W - NA_WIN_C)
    col_ok = (kc >= c0) & (kc < c0 + NA_WIN_C)
    ci = np.clip(kc - qc + NA_WIN_C - 1, 0, 2 * NA_WIN_C - 2)
    ai = np.arange(NA_WIN_R)[:, None] + np.arange(NA_WIN_R)[None, :]
    tab = rpb[:, ai[:, None, :, None], ci[None, :, None, :]]
    tab = jnp.where(jnp.asarray(col_ok)[None, None, :, None, :], tab.astype(F32), NEG_INF)
    h = rpb.shape[0]
    return tab.reshape(h, NA_WIN_R, GRID_W, NA_WIN_R * GRID_W)


NA_QROWS = 8
NA_QBLK = NA_QROWS * GRID_W
NA_KBLK = NA_QBLK // 2


def _na_kernel(kidx_ref, var_ref, q_ref, k0, k1, k2, k3, v0, v1, v2, v3, tab_ref, o_ref):
    n = pl.program_id(0)
    scale = HEAD_DIM ** -0.5
    q = q_ref[...]
    kspan = jnp.concatenate([k0[...], k1[...], k2[...], k3[...]], axis=0)
    vspan = jnp.concatenate([v0[...], v1[...], v2[...], v3[...]], axis=0)
    half = NA_WIN_R // 2
    span = NA_WIN_R * GRID_W

    def run(variant):
        for il in range(NA_QROWS):
            back = (min(il, half), half, max(il, half))[variant]
            start = il + half - back
            a0 = NA_WIN_R - 1 - back
            kw = kspan[start * GRID_W: start * GRID_W + span]
            vw = vspan[start * GRID_W: start * GRID_W + span]
            s = lax.dot_general(q[il * GRID_W:(il + 1) * GRID_W], kw, NT_DIMS,
                                preferred_element_type=F32) * scale + tab_ref[0, a0]
            m = jnp.max(s, axis=-1, keepdims=True)
            p = jnp.exp(s - m)
            l = jnp.sum(p, axis=-1, keepdims=True)
            o = jnp.dot(p.astype(BF16), vw, preferred_element_type=F32) / l
            o_ref[il * GRID_W:(il + 1) * GRID_W, :] = o.astype(o_ref.dtype)

    for variant in range(3):
        pl.when(var_ref[n] == variant)(functools.partial(run, variant))


def neighbourhood_attention(h, tab, seq_lens):
    m = h.shape[0]
    nblk = m // NA_QBLK
    kidx = np.zeros((4, nblk), np.int32)
    variant = np.ones((nblk,), np.int32)
    base = 0
    for t in seq_lens:
        nb = t // NA_QBLK
        assert nb >= 2 and t % NA_QBLK == 0
        lo, hi = 2 * base, 2 * (base + nb) - 1
        for n in range(nb):
            for j in range(4):
                kidx[j, base + n] = np.clip(2 * (base + n) - 1 + j, lo, hi)
        variant[base] = 0
        variant[base + nb - 1] = 2
        base += nb

    def kmap(j, col):
        return lambda n, hd, kidx_ref, var_ref: (kidx_ref[j, n], col + hd)

    kv_specs = [pl.BlockSpec((NA_KBLK, HEAD_DIM), kmap(j, COL_KA)) for j in range(4)]
    kv_specs += [pl.BlockSpec((NA_KBLK, HEAD_DIM), kmap(j, COL_VA)) for j in range(4)]
    grid_spec = pltpu.PrefetchScalarGridSpec(
        num_scalar_prefetch=2,
        grid=(nblk, NA_HEADS),
        in_specs=[pl.BlockSpec((NA_QBLK, HEAD_DIM), lambda n, hd, *_: (n, COL_QA + hd))]
        + kv_specs
        + [pl.BlockSpec((1,) + tab.shape[1:], lambda n, hd, *_: (hd, 0, 0, 0))],
        out_specs=pl.BlockSpec((NA_QBLK, HEAD_DIM), lambda n, hd, *_: (n, hd)),
    )
    return pl.pallas_call(
        _na_kernel,
        grid_spec=grid_spec,
        out_shape=jax.ShapeDtypeStruct((m, A_W), BF16),
        compiler_params=_cparams(("parallel", "arbitrary")),
        name="neighbourhood_attention",
    )(jnp.asarray(kidx), jnp.asarray(variant), h, *([h] * 8), tab)


def _diff_kernel(far_ref, q_ref, k_ref, v_ref, tiles_ref, lamv_ref, g_ref, o_ref,
                 m_ref, l_ref, acc_ref, *, nk, lambda_init):
    hd = pl.program_id(1)
    i = pl.program_id(2)
    t = DIFF_TILE
    q = q_ref[...]
    lane = lax.broadcasted_iota(jnp.int32, q.shape, 1)
    qk_scale = jnp.asarray(DIFF_QK_DIM ** -0.5, BF16)
    zero = jnp.zeros_like(q)
    qs = (jnp.where(lane < DIFF_QK_DIM, q, zero) * qk_scale,
          jnp.where(lane >= DIFF_QK_DIM, q, zero) * qk_scale)

    m_ref[...] = jnp.full_like(m_ref, NEG_INF)
    l_ref[...] = jnp.zeros_like(l_ref)
    acc_ref[...] = jnp.zeros_like(acc_ref)

    def chunk(j, load_bias):
        start = pl.multiple_of(j * t, t)
        k = k_ref[pl.ds(start, t), :]
        v = v_ref[pl.ds(start, t), :]
        bias = load_bias()
        for c in range(2):
            s = lax.dot_general(k, qs[c], NT_DIMS, preferred_element_type=F32) + bias
            m_old = m_ref[c]
            m_new = jnp.maximum(m_old, jnp.max(s, axis=0, keepdims=True))
            alpha = jnp.exp(m_old - m_new)
            p = jnp.exp(s - m_new)
            l_ref[c] = alpha * l_ref[c] + jnp.sum(p, axis=0, keepdims=True)
            pv = lax.dot_general(v, p.astype(BF16), TN_DIMS, preferred_element_type=F32)
            acc_ref[c] = alpha * acc_ref[c] + pv
            m_ref[c] = m_new

    def far_loop(lo, hi, side):
        def body(j, carry):
            chunk(j, lambda: far_ref[side, hd])
            return carry
        lax.fori_loop(lo, hi, body, 0)

    far_loop(0, jnp.maximum(i - 1, 0), 0)
    for o in (-1, 0, 1):
        j = i + o
        pl.when(jnp.logical_and(j >= 0, j < nk))(
            functools.partial(chunk, jnp.clip(j, 0, nk - 1), lambda o=o: tiles_ref[0, o + 1]))
    far_loop(jnp.minimum(i + 2, nk), nk, 1)

    lamv = lamv_ref[...]
    lam = (jnp.exp(jnp.sum(lamv[0:1] * lamv[1:2], axis=-1, keepdims=True))
           - jnp.exp(jnp.sum(lamv[2:3] * lamv[3:4], axis=-1, keepdims=True)) + lambda_init)
    o_t = acc_ref[0] / l_ref[0] - lam * (acc_ref[1] / l_ref[1])
    ms = jnp.mean(o_t * o_t, axis=0, keepdims=True)
    o_t = o_t * lax.rsqrt(ms + RMS_EPS) * g_ref[...] * (1.0 - lambda_init)
    o_ref[...] = o_t.T.astype(o_ref.dtype)


def differential_attention(h, row0, nb, t_seq, tiles, far, lamv, sub_g, lambda_init):
    t = DIFF_TILE
    nq = t_seq // t
    assert row0 % t_seq == 0 and t_seq % t == 0
    qb0, sb0 = row0 // t, row0 // t_seq
    return pl.pallas_call(
        functools.partial(_diff_kernel, nk=nq, lambda_init=lambda_init),
        grid=(nb, DIFF_HEADS, nq),
        in_specs=[
            pl.BlockSpec(memory_space=pltpu.SMEM),
            pl.BlockSpec((t, HEAD_DIM), lambda b, hd, i: (qb0 + b * nq + i, COL_QB + hd)),
            pl.BlockSpec((t_seq, HEAD_DIM), lambda b, hd, i: (sb0 + b, COL_KB + hd)),
            pl.BlockSpec((t_seq, HEAD_DIM), lambda b, hd, i: (sb0 + b, COL_VB + hd)),
            pl.BlockSpec((1, 3, t, t), lambda b, hd, i: (hd, 0, 0, 0)),
            pl.BlockSpec((4, DIFF_QK_DIM), lambda b, hd, i: (0, 0)),
            pl.BlockSpec((HEAD_DIM, 1), lambda b, hd, i: (0, 0)),
        ],
        out_specs=pl.BlockSpec((t, HEAD_DIM), lambda b, hd, i: (b * nq + i, hd)),
        scratch_shapes=[
            pltpu.VMEM((2, 1, t), F32),
            pltpu.VMEM((2, 1, t), F32),
            pltpu.VMEM((2, HEAD_DIM, t), F32),
        ],
        out_shape=jax.ShapeDtypeStruct((nb * t_seq, B_W), BF16),
        compiler_params=_cparams(("parallel", "parallel", "arbitrary")),
        name="differential_attention",
    )(far, h, h, h, tiles, lamv, sub_g.reshape(HEAD_DIM, 1).astype(F32))


def _swa_kernel(edge_ref, q0, q1, q2, kp, kc, kn, vp, vc, vn, bias_ref, sink_ref, o_ref):
    i = pl.program_id(0)
    j = pl.program_id(1)
    scale = HEAD_DIM ** -0.5
    w = SWA_TQ + 2 * SWA_WINDOW
    kwin = jnp.concatenate([kp[...], kc[...], kn[...]], axis=0)
    vwin = jnp.concatenate([vp[...], vc[...], vn[...]], axis=0)
    col = lax.broadcasted_iota(jnp.int32, (1, w), 1)
    is_first = edge_ref[0, i] == 1
    is_last = edge_ref[1, i] == 1
    valid = jnp.logical_and(jnp.logical_or(col >= SWA_WINDOW, jnp.logical_not(is_first)),
                            jnp.logical_or(col < SWA_TQ + SWA_WINDOW, jnp.logical_not(is_last)))
    for g, q_ref in enumerate((q0, q1, q2)):
        sink = sink_ref[j * SWA_GROUP + g]
        s = lax.dot_general(q_ref[...], kwin, NT_DIMS, preferred_element_type=F32) * scale + bias_ref[g]
        s = jnp.where(valid, s, NEG_INF)
        m = jnp.maximum(jnp.max(s, axis=-1, keepdims=True), sink)
        p = jnp.exp(s - m)
        l = jnp.sum(p, axis=-1, keepdims=True) + jnp.exp(sink - m)
        o = jnp.dot(p.astype(BF16), vwin, preferred_element_type=F32) / l
        o_ref[:, g * HEAD_DIM:(g + 1) * HEAD_DIM] = o.astype(o_ref.dtype)


def sliding_window_attention(h, bias, sink, seq_lens):
    m = h.shape[0]
    nblk = m // SWA_TQ
    ratio = SWA_TQ // SWA_WINDOW
    edge = np.zeros((2, nblk), np.int32)
    base = 0
    for t in seq_lens:
        assert t % SWA_TQ == 0
        edge[0, base] = 1
        base += t // SWA_TQ
        edge[1, base - 1] = 1
    last_halo = m // SWA_WINDOW - 1

    def qmap(g):
        return lambda i, j, *_: (i, COL_QC + j * SWA_GROUP + g)

    def halo(col):
        return [
            pl.BlockSpec((SWA_WINDOW, HEAD_DIM), lambda i, j, *_: (jnp.maximum(ratio * i - 1, 0), col + j)),
            pl.BlockSpec((SWA_TQ, HEAD_DIM), lambda i, j, *_: (i, col + j)),
            pl.BlockSpec((SWA_WINDOW, HEAD_DIM),
                         lambda i, j, *_: (jnp.minimum(ratio * (i + 1), last_halo), col + j)),
        ]

    grid_spec = pltpu.PrefetchScalarGridSpec(
        num_scalar_prefetch=1,
        grid=(nblk, SWA_KV_HEADS),
        in_specs=[pl.BlockSpec((SWA_TQ, HEAD_DIM), qmap(g)) for g in range(SWA_GROUP)]
        + halo(COL_KC) + halo(COL_VC)
        + [pl.BlockSpec((SWA_GROUP,) + bias.shape[1:], lambda i, j, *_: (j, 0, 0)),
           pl.BlockSpec(memory_space=pltpu.SMEM)],
        out_specs=pl.BlockSpec((SWA_TQ, SWA_GROUP * HEAD_DIM), lambda i, j, *_: (i, j)),
    )
    return pl.pallas_call(
        _swa_kernel,
        grid_spec=grid_spec,
        out_shape=jax.ShapeDtypeStruct((m, C_Q_W), BF16),
        compiler_params=_cparams(("parallel", "arbitrary")),
        name="sliding_window_attention",
    )(jnp.asarray(edge), *([h] * 9), bias, sink.astype(F32))


def _pad_ff(w, axis):
    pad = [(0, 0), (0, 0)]
    pad[axis] = (0, D_FF_PAD - D_FF)
    return jnp.pad(w.astype(BF16), pad)


def encoder_trunk(x, p, seq_groups, w_in, w_out, na_rpb, lam_q1, lam_k1, lam_q2, lam_k2,
                  diff_g, sink, rel_bias, ffn1_gate, ffn1_up, ffn1_down,
                  ffn2_gate, ffn2_up, ffn2_down, ple_gate, ple_proj, ln_g, ln_b):
    seq_lens = [t for _, nb, t in seq_groups for _ in range(nb)]
    diff_tiles, diff_far = diff_bias_tables(rel_bias[:, :DIFF_HEADS])
    diff_far = diff_far.astype(F32)
    swa_bias = swa_bias_table(rel_bias[:, DIFF_HEADS:])
    xb = x.astype(BF16)
    pb = p.astype(BF16)
    for li in range(DEPTH):
        ffn_tk = D_FF_PAD // 4
        hid = gateup(xb, _pad_ff(ffn1_gate[li], 1), _pad_ff(ffn1_up[li], 1))
        r = matmul(hid, _pad_ff(ffn1_down[li], 0), F32, tk=ffn_tk)
        x, xb = residual_layernorm(x, r, ln_g[li, 0], ln_b[li, 0], DEEPNORM_ALPHA, 0.5)
        h = matmul(xb, w_in[li].astype(BF16), BF16)
        a_out = neighbourhood_attention(h, na_bias_table(na_rpb[li]), seq_lens)
        lambda_init = 0.8 - 0.6 * math.exp(-0.3 * li)
        lamv = jnp.stack([lam_q1[li], lam_k1[li], lam_q2[li], lam_k2[li]]).astype(F32)
        b_out = jnp.concatenate([
            differential_attention(h, row0, nb, t, diff_tiles, diff_far, lamv, diff_g[li], lambda_init)
            for row0, nb, t in seq_groups], axis=0)
        c_out = sliding_window_attention(h, swa_bias, sink[li], seq_lens)
        mix = jnp.concatenate([a_out, b_out, c_out], axis=-1)
        r = matmul(mix, w_out[li].astype(BF16), F32)
        x, xb = residual_layernorm(x, r, ln_g[li, 1], ln_b[li, 1], DEEPNORM_ALPHA, 1.0)
        res = ple_residual(x, xb, pb[li], ple_gate[li].astype(BF16), ple_proj[li].astype(BF16))
        hid = gateup(xb, _pad_ff(ffn2_gate[li], 1), _pad_ff(ffn2_up[li], 1))
        r = matmul(hid, _pad_ff(ffn2_down[li], 0), F32, tk=ffn_tk)
        x, xb = residual_layernorm(res, r, ln_g[li, 2], ln_b[li, 2], 1.0, 0.5)
    return x


def kernel(x_prompt, x_sample, p_prompt, p_sample, w_in, w_out, na_rpb, lam_q1, lam_k1, lam_q2, lam_k2,
           diff_g, sink, rel_bias, ffn1_gate, ffn1_up, ffn1_down, ffn2_gate, ffn2_up, ffn2_down,
           ple_gate, ple_proj, ln_g, ln_b):
    bp, tp, d = x_prompt.shape
    bs, ts, _ = x_sample.shape
    mp, ms = bp * tp, bs * ts
    x = jnp.concatenate([x_prompt.reshape(mp, d), x_sample.reshape(ms, d)], axis=0)
    p = jnp.concatenate([p_prompt.reshape(DEPTH, mp, PLE_DIM), p_sample.reshape(DEPTH, ms, PLE_DIM)], axis=1)
    seq_groups = [(0, bp, tp), (mp, bs, ts)]
    y = encoder_trunk(x, p, seq_groups, w_in, w_out, na_rpb, lam_q1, lam_k1, lam_q2, lam_k2,
                      diff_g, sink, rel_bias, ffn1_gate, ffn1_up, ffn1_down,
                      ffn2_gate, ffn2_up, ffn2_down, ple_gate, ple_proj, ln_g, ln_b)
    return y[:mp].reshape(bp, tp, d), y[mp:].reshape(bs, ts, d)
```

```python
import functools
import math

import jax
import jax.numpy as jnp
import numpy as np
from jax import lax
from jax.experimental import pallas as pl
from jax.experimental.pallas import tpu as pltpu

F32 = jnp.float32
BF16 = jnp.bfloat16

D_MODEL = 4096
DEPTH = 2
GRID_W = 64
HEAD_DIM = 128
NA_HEADS = 8
NA_WIN_R = 8
NA_WIN_C = 16
DIFF_HEADS = 12
DIFF_QK_DIM = 64
SWA_Q_HEADS = 12
SWA_KV_HEADS = 4
SWA_GROUP = SWA_Q_HEADS // SWA_KV_HEADS
SWA_WINDOW = 128
REL_BUCKETS = 32
REL_MAX_DIST = 128
D_FF = 11008
PLE_DIM = 256
LN_EPS = 1e-5
RMS_EPS = 1e-5
NEG_INF = -1e30
DEEPNORM_ALPHA = (2 * DEPTH) ** 0.25

A_W = NA_HEADS * HEAD_DIM
B_W = DIFF_HEADS * HEAD_DIM
C_Q_W = SWA_Q_HEADS * HEAD_DIM
C_KV_W = SWA_KV_HEADS * HEAD_DIM
MIX_W = A_W + B_W + C_Q_W
IN_W = 3 * A_W + 3 * B_W + C_Q_W + 2 * C_KV_W
COL_QA, COL_KA, COL_VA = 0, A_W // 128, 2 * A_W // 128
COL_QB = 3 * A_W // 128
COL_KB = COL_QB + B_W // 128
COL_VB = COL_KB + B_W // 128
COL_QC = COL_VB + B_W // 128
COL_KC = COL_QC + C_Q_W // 128
COL_VC = COL_KC + C_KV_W // 128
MIX_COL_C, MIX_COL_A, MIX_COL_B = 0, C_Q_W // 128, (C_Q_W + A_W) // 128

V7X_VMEM_BYTES = 64 * 1024 * 1024
VMEM_LIMIT = 56 * 1024 * 1024

FF_TILE = 512
D_FF_PAD = -(-D_FF // 1024) * 1024

NT_DIMS = (((1,), (1,)), ((), ()))


def _cparams(sem):
    return pltpu.CompilerParams(dimension_semantics=sem, vmem_limit_bytes=VMEM_LIMIT)


def _cast_kernel(w_ref, o_ref, *, rows, cols, masked):
    x = w_ref[0]
    if masked:
        tr, tc = x.shape
        r = pl.program_id(1) * tr + lax.broadcasted_iota(jnp.int32, x.shape, 0)
        c = pl.program_id(2) * tc + lax.broadcasted_iota(jnp.int32, x.shape, 1)
        x = jnp.where(jnp.logical_and(r < rows, c < cols), x, 0.0)
    o_ref[0] = x.astype(o_ref.dtype)


def cast_weights(w, rows_out=None, cols_out=None, row_rotate=0):
    nl, rows, cols = w.shape
    rows_out = rows if rows_out is None else rows_out
    cols_out = cols if cols_out is None else cols_out
    tr = math.gcd(rows_out, 512)
    tc = math.gcd(cols_out, 1024)
    nrb = rows_out // tr
    assert row_rotate % tr == 0 and (row_rotate == 0 or rows_out == rows)
    shift = row_rotate // tr
    spec = pl.BlockSpec((1, tr, tc), lambda l, i, j: (l, i, j))
    return pl.pallas_call(
        functools.partial(_cast_kernel, rows=rows, cols=cols,
                          masked=(rows_out != rows or cols_out != cols)),
        grid=(nl, nrb, cols_out // tc),
        in_specs=[pl.BlockSpec((1, tr, tc), lambda l, i, j: (l, (i + shift) % nrb, j))],
        out_specs=spec,
        out_shape=jax.ShapeDtypeStruct((nl, rows_out, cols_out), BF16),
        compiler_params=_cparams(("parallel", "parallel", "parallel")),
        name="cast_weights",
    )(w)


def _gateup_kernel(x_ref, wg_ref, wu_ref, o_ref):
    x = x_ref[...]
    g = jnp.dot(x, wg_ref[...], preferred_element_type=F32)
    u = jnp.dot(x, wu_ref[...], preferred_element_type=F32)
    o_ref[...] = (g * jax.nn.sigmoid(g) * u).astype(o_ref.dtype)


def gateup(x, wg, wu):
    m, k = x.shape
    f = wg.shape[1]
    tm = min(1024, m)
    tn = min(FF_TILE, f)
    return pl.pallas_call(
        _gateup_kernel,
        grid=(m // tm, f // tn),
        in_specs=[
            pl.BlockSpec((tm, k), lambda i, j: (i, 0)),
            pl.BlockSpec((k, tn), lambda i, j: (0, j)),
            pl.BlockSpec((k, tn), lambda i, j: (0, j)),
        ],
        out_specs=pl.BlockSpec((tm, tn), lambda i, j: (i, j)),
        out_shape=jax.ShapeDtypeStruct((m, f), BF16),
        compiler_params=_cparams(("parallel", "arbitrary")),
        name="gateup",
    )(x, wg, wu)


def _matmul_kernel(a_ref, w_ref, o_ref, acc_ref, *, nk):
    kk = pl.program_id(2)
    part = jnp.dot(a_ref[...], w_ref[...], preferred_element_type=F32)
    if nk == 1:
        o_ref[...] = part.astype(o_ref.dtype)
        return

    @pl.when(kk == 0)
    def _():
        acc_ref[...] = part

    @pl.when(jnp.logical_and(kk > 0, kk < nk - 1))
    def _():
        acc_ref[...] += part

    @pl.when(kk == nk - 1)
    def _():
        o_ref[...] = (acc_ref[...] + part).astype(o_ref.dtype)


def matmul(a, w, out_dtype, tk=None):
    m, k = a.shape
    n = w.shape[1]
    tm = min(1024, m)
    tn = min(1024, n)
    tk = k if tk is None else tk
    nk = k // tk
    return pl.pallas_call(
        functools.partial(_matmul_kernel, nk=nk),
        grid=(m // tm, n // tn, nk),
        in_specs=[
            pl.BlockSpec((tm, tk), lambda i, j, kk: (i, kk)),
            pl.BlockSpec((tk, tn), lambda i, j, kk: (kk, j)),
        ],
        out_specs=pl.BlockSpec((tm, tn), lambda i, j, kk: (i, j)),
        out_shape=jax.ShapeDtypeStruct((m, n), out_dtype),
        scratch_shapes=[pltpu.VMEM((tm, tn) if nk > 1 else (8, 128), F32)],
        compiler_params=_cparams(("parallel", "parallel", "arbitrary")),
        name="matmul",
    )(a, w)


def _ple_kernel(x_ref, xb_ref, p_ref, wg_ref, wp_ref, o_ref):
    gate = jax.nn.sigmoid(jnp.dot(xb_ref[...], wg_ref[...], preferred_element_type=F32))
    proj = jnp.dot(p_ref[...], wp_ref[...], preferred_element_type=F32)
    o_ref[...] = DEEPNORM_ALPHA * x_ref[...] + gate * proj


def ple_residual(x, xb, p, wg, wp):
    m, d = x.shape
    tm = min(512, m)
    tn = min(1024, d)
    return pl.pallas_call(
        _ple_kernel,
        grid=(m // tm, d // tn),
        in_specs=[
            pl.BlockSpec((tm, tn), lambda i, j: (i, j)),
            pl.BlockSpec((tm, d), lambda i, j: (i, 0)),
            pl.BlockSpec((tm, p.shape[1]), lambda i, j: (i, 0)),
            pl.BlockSpec((d, tn), lambda i, j: (0, j)),
            pl.BlockSpec((p.shape[1], tn), lambda i, j: (0, j)),
        ],
        out_specs=pl.BlockSpec((tm, tn), lambda i, j: (i, j)),
        out_shape=jax.ShapeDtypeStruct((m, d), F32),
        compiler_params=_cparams(("parallel", "arbitrary")),
        name="ple_residual",
    )(x, xb, p, wg, wp)


def _ln_kernel(res_ref, r_ref, g_ref, b_ref, of_ref, ob_ref, *, res_scale, r_scale):
    y = res_scale * res_ref[...] + r_scale * r_ref[...]
    mu = jnp.mean(y, axis=-1, keepdims=True)
    yc = y - mu
    var = jnp.mean(yc * yc, axis=-1, keepdims=True)
    out = yc * lax.rsqrt(var + LN_EPS) * g_ref[...] + b_ref[...]
    of_ref[...] = out
    ob_ref[...] = out.astype(BF16)


def residual_layernorm(res, r, g, b, res_scale, r_scale):
    m, d = res.shape
    tm = min(256, m)
    row = pl.BlockSpec((tm, d), lambda i: (i, 0))
    vec = pl.BlockSpec((1, d), lambda i: (0, 0))
    return pl.pallas_call(
        functools.partial(_ln_kernel, res_scale=res_scale, r_scale=r_scale),
        grid=(m // tm,),
        in_specs=[row, row, vec, vec],
        out_specs=[row, row],
        out_shape=[jax.ShapeDtypeStruct((m, d), F32), jax.ShapeDtypeStruct((m, d), BF16)],
        compiler_params=_cparams(("parallel",)),
        name="residual_layernorm",
    )(res, r, g.reshape(1, d), b.reshape(1, d))


def _t5_bucket(rel):
    half = REL_BUCKETS // 2
    max_exact = half // 2
    n = jnp.abs(rel)
    side = jnp.where(rel > 0, half, 0)
    nf = jnp.maximum(n, 1).astype(F32)
    large = max_exact + (jnp.log(nf / max_exact) / math.log(REL_MAX_DIST / max_exact)
                         * (half - max_exact)).astype(jnp.int32)
    large = jnp.minimum(large, half - 1)
    return side + jnp.where(n < max_exact, n, large)


DIFF_TILE = 256
DIFF_EXP_UNROLL = True
SWA_TQ = 256
LOG2E = math.log2(math.e)


def _bucket_select(bucket, table_ref, col, init):
    out = jnp.full(bucket.shape, init, F32)
    for b in range(REL_BUCKETS):
        out = jnp.where(bucket == b, table_ref[b, col], out)
    return out


NA_QROWS = 8
NA_QBLK = NA_QROWS * GRID_W
NA_KBLK = NA_QBLK // 2
NA_RPB_C = 2 * NA_WIN_C - 1


def _na_build_bias(rpb_ref, tab_ref, hd):
    shape = (GRID_W, 2 * GRID_W)
    lane = lax.broadcasted_iota(jnp.int32, shape, 1)
    qc = lax.broadcasted_iota(jnp.int32, shape, 0)
    second = lane >= GRID_W
    kc = jnp.where(second, lane - GRID_W, lane)
    d = kc - qc + (NA_WIN_C - 1)
    c0 = jnp.clip(qc - NA_WIN_C // 2, 0, GRID_W - NA_WIN_C)
    col_ok = jnp.logical_and(kc >= c0, kc < c0 + NA_WIN_C)
    pairs = []
    for r in range(2 * NA_WIN_R - 2):
        t2 = jnp.full(shape, NEG_INF, F32)
        for dd in range(NA_RPB_C):
            val = jnp.where(second, rpb_ref[hd, (r + 1) * NA_RPB_C + dd], rpb_ref[hd, r * NA_RPB_C + dd])
            t2 = jnp.where(d == dd, val, t2)
        pairs.append(jnp.where(col_ok, t2, NEG_INF))
    for a0 in range(NA_WIN_R):
        for jj in range(0, NA_WIN_R, 2):
            tab_ref[hd, a0, :, jj * GRID_W:(jj + 2) * GRID_W] = pairs[a0 + jj]


def _na_kernel(kidx_ref, var_ref, q_ref, k0, k1, k2, k3, v0, v1, v2, v3, rpb_ref, mix_ref, o_ref, tab_ref):
    del mix_ref
    n = pl.program_id(0)
    hd = pl.program_id(1)
    scale = HEAD_DIM ** -0.5

    pl.when(n == 0)(functools.partial(_na_build_bias, rpb_ref, tab_ref, hd))

    q = q_ref[...]
    kspan = jnp.concatenate([k0[...], k1[...], k2[...], k3[...]], axis=0)
    vspan = jnp.concatenate([v0[...], v1[...], v2[...], v3[...]], axis=0)
    half = NA_WIN_R // 2
    span = NA_WIN_R * GRID_W

    def run(variant):
        for il in range(NA_QROWS):
            back = (min(il, half), half, max(il, half))[variant]
            start = il + half - back
            a0 = NA_WIN_R - 1 - back
            kw = kspan[start * GRID_W: start * GRID_W + span]
            vw = vspan[start * GRID_W: start * GRID_W + span]
            s = lax.dot_general(q[il * GRID_W:(il + 1) * GRID_W], kw, NT_DIMS,
                                preferred_element_type=F32) * scale + tab_ref[hd, a0]
            m = jnp.max(s, axis=-1, keepdims=True)
            p = jnp.exp(s - m)
            l = jnp.sum(p, axis=-1, keepdims=True)
            o = jnp.dot(p.astype(BF16), vw, preferred_element_type=F32) / l
            o_ref[il * GRID_W:(il + 1) * GRID_W, :] = o.astype(o_ref.dtype)

    for variant in range(3):
        pl.when(var_ref[n] == variant)(functools.partial(run, variant))


def neighbourhood_attention(h, mix, rpb, seq_lens):
    m = h.shape[0]
    nblk = m // NA_QBLK
    kidx = np.zeros((4, nblk), np.int32)
    variant = np.ones((nblk,), np.int32)
    base = 0
    for t in seq_lens:
        nb = t // NA_QBLK
        assert nb >= 2 and t % NA_QBLK == 0
        lo, hi = 2 * base, 2 * (base + nb) - 1
        for n in range(nb):
            for j in range(4):
                kidx[j, base + n] = np.clip(2 * (base + n) - 1 + j, lo, hi)
        variant[base] = 0
        variant[base + nb - 1] = 2
        base += nb

    def kmap(j, col):
        return lambda n, hd, kidx_ref, var_ref: (kidx_ref[j, n], col + hd)

    kv_specs = [pl.BlockSpec((NA_KBLK, HEAD_DIM), kmap(j, COL_KA)) for j in range(4)]
    kv_specs += [pl.BlockSpec((NA_KBLK, HEAD_DIM), kmap(j, COL_VA)) for j in range(4)]
    grid_spec = pltpu.PrefetchScalarGridSpec(
        num_scalar_prefetch=2,
        grid=(nblk, NA_HEADS),
        in_specs=[pl.BlockSpec((NA_QBLK, HEAD_DIM), lambda n, hd, *_: (n, COL_QA + hd))]
        + kv_specs
        + [pl.BlockSpec(memory_space=pltpu.SMEM), pl.BlockSpec(memory_space=pl.ANY)],
        out_specs=pl.BlockSpec((NA_QBLK, HEAD_DIM), lambda n, hd, *_: (n, MIX_COL_A + hd)),
        scratch_shapes=[pltpu.VMEM((NA_HEADS, NA_WIN_R, GRID_W, NA_WIN_R * GRID_W), F32)],
    )
    return pl.pallas_call(
        _na_kernel,
        grid_spec=grid_spec,
        out_shape=jax.ShapeDtypeStruct(mix.shape, mix.dtype),
        input_output_aliases={12: 0},
        compiler_params=_cparams(("arbitrary", "arbitrary")),
        name="neighbourhood_attention",
    )(jnp.asarray(kidx), jnp.asarray(variant), h, *([h] * 8),
      rpb.reshape(NA_HEADS, -1).astype(F32), mix)


def _diff_kernel(far_ref, rel_ref, q_ref, k_ref, v_ref, bkt_ref, lamv_ref, g_ref, mix_ref, o_ref,
                 s_ref, p_ref, vt_ref, bias_ref, *, nk, lambda_init):
    del mix_ref
    hd = pl.program_id(1)
    i = pl.program_id(2)
    t = DIFF_TILE

    @pl.when(i == 0)
    def _prepare_head():
        for o in range(3):
            bias_ref[o] = _bucket_select(bkt_ref[o], rel_ref, hd, 0.0) * LOG2E
        for c in range(nk):
            vt_ref[:, c * t:(c + 1) * t] = v_ref[c * t:(c + 1) * t, :].astype(F32).T.astype(BF16)

    q = q_ref[...]
    lane = lax.broadcasted_iota(jnp.int32, q.shape, 1)
    qk_scale = jnp.asarray(DIFF_QK_DIM ** -0.5, BF16)
    zero = jnp.zeros_like(q)
    qs = (jnp.where(lane < DIFF_QK_DIM, q, zero) * qk_scale,
          jnp.where(lane >= DIFF_QK_DIM, q, zero) * qk_scale)

    c_lo = rel_ref[far_ref[0], hd] * LOG2E
    c_hi = rel_ref[far_ref[1], hd] * LOG2E

    def far_bias(j, near):
        return jnp.where(j < i - 1, c_lo, jnp.where(j > i + 1, c_hi, near))

    m = []
    for c in range(2):
        s = lax.dot_general(k_ref[...], qs[c], NT_DIMS, preferred_element_type=F32) * LOG2E
        s_ref[c] = s
        mc = jnp.full((1, t), NEG_INF, F32)
        for jc in range(nk):
            mc = jnp.maximum(mc, jnp.max(s[jc * t:(jc + 1) * t], axis=0, keepdims=True) + far_bias(jc, NEG_INF))
        m.append(mc)

    for o in (-1, 0, 1):
        j = i + o
        ok = jnp.logical_and(j >= 0, j < nk)
        row = pl.multiple_of(jnp.clip(j, 0, nk - 1) * t, t)
        tile = jnp.where(ok, bias_ref[o + 1], 0.0)
        for c in range(2):
            blk = s_ref[c, pl.ds(row, t), :] + tile
            s_ref[c, pl.ds(row, t), :] = blk
            m[c] = jnp.maximum(m[c], jnp.max(blk, axis=0, keepdims=True) + jnp.where(ok, 0.0, NEG_INF))

    def exp_body(j, l):
        row = pl.multiple_of(j * t, t)
        sh = far_bias(j, 0.0)
        out = []
        for c in range(2):
            p = jnp.exp2(s_ref[c, pl.ds(row, t), :] - (m[c] - sh))
            p_ref[c, pl.ds(row, t), :] = p.astype(BF16)
            out.append(l[c] + jnp.sum(p, axis=0, keepdims=True))
        return tuple(out)

    zeros = jnp.zeros((1, t), F32)
    l = lax.fori_loop(0, nk, exp_body, (zeros, zeros), unroll=DIFF_EXP_UNROLL)

    acc = [jnp.dot(vt_ref[...], p_ref[c], preferred_element_type=F32) for c in range(2)]
    lamv = lamv_ref[...]
    lam = (jnp.exp(jnp.sum(lamv[0:1] * lamv[1:2], axis=-1, keepdims=True))
           - jnp.exp(jnp.sum(lamv[2:3] * lamv[3:4], axis=-1, keepdims=True)) + lambda_init)
    o_t = acc[0] / l[0] - lam * (acc[1] / l[1])
    ms = jnp.mean(o_t * o_t, axis=0, keepdims=True)
    o_t = o_t * lax.rsqrt(ms + RMS_EPS) * g_ref[...] * (1.0 - lambda_init)
    o_ref[...] = o_t.T.astype(o_ref.dtype)


def differential_attention(h, mix, row0, nb, t_seq, rel_bias, lamv, sub_g, lambda_init):
    t = DIFF_TILE
    nq = t_seq // t
    assert row0 % t_seq == 0 and t_seq % t == 0
    qb0, sb0 = row0 // t, row0 // t_seq
    kq = np.arange(t)[:, None] - np.arange(t)[None, :]
    bkt = _t5_bucket(jnp.asarray(np.stack([kq + o * t for o in (-1, 0, 1)]), dtype=jnp.int32))
    far = _t5_bucket(jnp.asarray([-2 * t, 2 * t], dtype=jnp.int32))
    smem = pl.BlockSpec(memory_space=pltpu.SMEM)
    return pl.pallas_call(
        functools.partial(_diff_kernel, nk=nq, lambda_init=lambda_init),
        grid=(nb, DIFF_HEADS, nq),
        in_specs=[
            smem, smem,
            pl.BlockSpec((t, HEAD_DIM), lambda b, hd, i: (qb0 + b * nq + i, COL_QB + hd)),
            pl.BlockSpec((t_seq, HEAD_DIM), lambda b, hd, i: (sb0 + b, COL_KB + hd)),
            pl.BlockSpec((t_seq, HEAD_DIM), lambda b, hd, i: (sb0 + b, COL_VB + hd)),
            pl.BlockSpec((3, t, t), lambda b, hd, i: (0, 0, 0)),
            pl.BlockSpec((4, DIFF_QK_DIM), lambda b, hd, i: (0, 0)),
            pl.BlockSpec((HEAD_DIM, 1), lambda b, hd, i: (0, 0)),
            pl.BlockSpec(memory_space=pl.ANY),
        ],
        out_specs=pl.BlockSpec((t, HEAD_DIM), lambda b, hd, i: (qb0 + b * nq + i, MIX_COL_B + hd)),
        scratch_shapes=[
            pltpu.VMEM((2, t_seq, t), F32),
            pltpu.VMEM((2, t_seq, t), BF16),
            pltpu.VMEM((HEAD_DIM, t_seq), BF16),
            pltpu.VMEM((3, t, t), F32),
        ],
        out_shape=jax.ShapeDtypeStruct(mix.shape, mix.dtype),
        input_output_aliases={8: 0},
        compiler_params=_cparams(("arbitrary", "arbitrary", "arbitrary")),
        name="differential_attention",
    )(far, rel_bias.astype(F32), h, h, h, bkt, lamv, sub_g.reshape(HEAD_DIM, 1).astype(F32), mix)


def _swa_kernel(edge_ref, q0, q1, q2, kp, kc, kn, vp, vc, vn, bkt_ref, rel_ref, sink_ref, mix_ref, o_ref,
                bias_ref):
    del mix_ref
    i = pl.program_id(0)
    j = pl.program_id(1)
    scale = HEAD_DIM ** -0.5
    w = SWA_TQ + 2 * SWA_WINDOW

    @pl.when(i == 0)
    def _build_bias():
        for g in range(SWA_GROUP):
            head = j * SWA_GROUP + g
            bias_ref[head] = _bucket_select(bkt_ref[...], rel_ref, DIFF_HEADS + head, NEG_INF)

    kwin = jnp.concatenate([kp[...], kc[...], kn[...]], axis=0)
    vwin = jnp.concatenate([vp[...], vc[...], vn[...]], axis=0)
    col = lax.broadcasted_iota(jnp.int32, (1, w), 1)
    is_first = edge_ref[0, i] == 1
    is_last = edge_ref[1, i] == 1
    valid = jnp.logical_and(jnp.logical_or(col >= SWA_WINDOW, jnp.logical_not(is_first)),
                            jnp.logical_or(col < SWA_TQ + SWA_WINDOW, jnp.logical_not(is_last)))
    for g, q_ref in enumerate((q0, q1, q2)):
        head = j * SWA_GROUP + g
        sink = sink_ref[head]
        s = lax.dot_general(q_ref[...], kwin, NT_DIMS, preferred_element_type=F32) * scale + bias_ref[head]
        s = jnp.where(valid, s, NEG_INF)
        m = jnp.maximum(jnp.max(s, axis=-1, keepdims=True), sink)
        p = jnp.exp(s - m)
        l = jnp.sum(p, axis=-1, keepdims=True) + jnp.exp(sink - m)
        o = jnp.dot(p.astype(BF16), vwin, preferred_element_type=F32) / l
        o_ref[:, g * HEAD_DIM:(g + 1) * HEAD_DIM] = o.astype(o_ref.dtype)


def sliding_window_attention(h, mix, rel_bias, sink, seq_lens):
    m = h.shape[0]
    nblk = m // SWA_TQ
    ratio = SWA_TQ // SWA_WINDOW
    edge = np.zeros((2, nblk), np.int32)
    base = 0
    for t in seq_lens:
        assert t % SWA_TQ == 0
        edge[0, base] = 1
        base += t // SWA_TQ
        edge[1, base - 1] = 1
    last_halo = m // SWA_WINDOW - 1
    w = SWA_TQ + 2 * SWA_WINDOW
    rel = (np.arange(w)[None, :] - SWA_WINDOW) - np.arange(SWA_TQ)[:, None]
    bkt = jnp.where(jnp.asarray(np.abs(rel) <= SWA_WINDOW), _t5_bucket(jnp.asarray(rel, dtype=jnp.int32)), -1)

    def qmap(g):
        return lambda i, j, *_: (i, COL_QC + j * SWA_GROUP + g)

    def halo(col):
        return [
            pl.BlockSpec((SWA_WINDOW, HEAD_DIM), lambda i, j, *_: (jnp.maximum(ratio * i - 1, 0), col + j)),
            pl.BlockSpec((SWA_TQ, HEAD_DIM), lambda i, j, *_: (i, col + j)),
            pl.BlockSpec((SWA_WINDOW, HEAD_DIM),
                         lambda i, j, *_: (jnp.minimum(ratio * (i + 1), last_halo), col + j)),
        ]

    smem = pl.BlockSpec(memory_space=pltpu.SMEM)
    grid_spec = pltpu.PrefetchScalarGridSpec(
        num_scalar_prefetch=1,
        grid=(nblk, SWA_KV_HEADS),
        in_specs=[pl.BlockSpec((SWA_TQ, HEAD_DIM), qmap(g)) for g in range(SWA_GROUP)]
        + halo(COL_KC) + halo(COL_VC)
        + [pl.BlockSpec((SWA_TQ, w), lambda i, j, *_: (0, 0)), smem, smem,
           pl.BlockSpec(memory_space=pl.ANY)],
        out_specs=pl.BlockSpec((SWA_TQ, SWA_GROUP * HEAD_DIM),
                               lambda i, j, *_: (i, MIX_COL_C // SWA_GROUP + j)),
        scratch_shapes=[pltpu.VMEM((SWA_Q_HEADS, SWA_TQ, w), F32)],
    )
    assert MIX_COL_C % SWA_GROUP == 0
    return pl.pallas_call(
        _swa_kernel,
        grid_spec=grid_spec,
        out_shape=jax.ShapeDtypeStruct(mix.shape, mix.dtype),
        input_output_aliases={13: 0},
        compiler_params=_cparams(("arbitrary", "arbitrary")),
        name="sliding_window_attention",
    )(jnp.asarray(edge), *([h] * 9), bkt, rel_bias.astype(F32), sink.astype(F32), mix)


def encoder_trunk(x, p, seq_groups, w_in, w_out, na_rpb, lam_q1, lam_k1, lam_q2, lam_k2,
                  diff_g, sink, rel_bias, ffn1_gate, ffn1_up, ffn1_down,
                  ffn2_gate, ffn2_up, ffn2_down, ple_gate, ple_proj, ln_g, ln_b):
    m = x.shape[0]
    seq_lens = [t for _, nb, t in seq_groups for _ in range(nb)]
    xb = x.astype(BF16)
    pb = p.astype(BF16)
    w_in, ple_gate, ple_proj = (cast_weights(w) for w in (w_in, ple_gate, ple_proj))
    w_out = cast_weights(w_out, row_rotate=A_W + B_W)
    ffn1_gate, ffn1_up, ffn2_gate, ffn2_up = (
        cast_weights(w, cols_out=D_FF_PAD) for w in (ffn1_gate, ffn1_up, ffn2_gate, ffn2_up))
    ffn1_down, ffn2_down = (cast_weights(w, rows_out=D_FF_PAD) for w in (ffn1_down, ffn2_down))
    ffn_tk = D_FF_PAD // 4
    for li in range(DEPTH):
        hid = gateup(xb, ffn1_gate[li], ffn1_up[li])
        r = matmul(hid, ffn1_down[li], F32, tk=ffn_tk)
        x, xb = residual_layernorm(x, r, ln_g[li, 0], ln_b[li, 0], DEEPNORM_ALPHA, 0.5)
        h = matmul(xb, w_in[li], BF16)
        lambda_init = 0.8 - 0.6 * math.exp(-0.3 * li)
        lamv = jnp.stack([lam_q1[li], lam_k1[li], lam_q2[li], lam_k2[li]]).astype(F32)
        mix = jnp.zeros((m, MIX_W), BF16)
        mix = neighbourhood_attention(h, mix, na_rpb[li], seq_lens)
        for row0, nb, t in seq_groups:
            mix = differential_attention(h, mix, row0, nb, t, rel_bias, lamv, diff_g[li], lambda_init)
        mix = sliding_window_attention(h, mix, rel_bias, sink[li], seq_lens)
        r = matmul(mix, w_out[li], F32)
        x, xb = residual_layernorm(x, r, ln_g[li, 1], ln_b[li, 1], DEEPNORM_ALPHA, 1.0)
        res = ple_residual(x, xb, pb[li], ple_gate[li], ple_proj[li])
        hid = gateup(xb, ffn2_gate[li], ffn2_up[li])
        r = matmul(hid, ffn2_down[li], F32, tk=ffn_tk)
        x, xb = residual_layernorm(res, r, ln_g[li, 2], ln_b[li, 2], 1.0, 0.5)
    return x


def kernel(x_prompt, x_sample, p_prompt, p_sample, w_in, w_out, na_rpb, lam_q1, lam_k1, lam_q2, lam_k2,
           diff_g, sink, rel_bias, ffn1_gate, ffn1_up, ffn1_down, ffn2_gate, ffn2_up, ffn2_down,
           ple_gate, ple_proj, ln_g, ln_b):
    bp, tp, d = x_prompt.shape
    bs, ts, _ = x_sample.shape
    mp, ms = bp * tp, bs * ts
    x = jnp.concatenate([x_prompt.reshape(mp, d), x_sample.reshape(ms, d)], axis=0)
    p = jnp.concatenate([p_prompt.reshape(DEPTH, mp, PLE_DIM), p_sample.reshape(DEPTH, ms, PLE_DIM)], axis=1)
    seq_groups = [(0, bp, tp), (mp, bs, ts)]
    y = encoder_trunk(x, p, seq_groups, w_in, w_out, na_rpb, lam_q1, lam_k1, lam_q2, lam_k2,
                      diff_g, sink, rel_bias, ffn1_gate, ffn1_up, ffn1_down,
                      ffn2_gate, ffn2_up, ffn2_down, ple_gate, ple_proj, ln_g, ln_b)
    return y[:mp].reshape(bp, tp, d), y[mp:].reshape(bs, ts, d)
```

```python
import functools
import math

import jax
import jax.numpy as jnp
import numpy as np
from jax import lax
from jax.experimental import pallas as pl
from jax.experimental.pallas import tpu as pltpu

F32 = jnp.float32
BF16 = jnp.bfloat16

D_MODEL = 4096
DEPTH = 2
GRID_W = 64
HEAD_DIM = 128
NA_HEADS = 8
NA_WIN_R = 8
NA_WIN_C = 16
DIFF_HEADS = 12
DIFF_QK_DIM = 64
SWA_Q_HEADS = 12
SWA_KV_HEADS = 4
SWA_GROUP = SWA_Q_HEADS // SWA_KV_HEADS
SWA_WINDOW = 128
REL_BUCKETS = 32
REL_MAX_DIST = 128
D_FF = 11008
PLE_DIM = 256
LN_EPS = 1e-5
RMS_EPS = 1e-5
NEG_INF = -1e30
DEEPNORM_ALPHA = (2 * DEPTH) ** 0.25

A_W = NA_HEADS * HEAD_DIM
B_W = DIFF_HEADS * HEAD_DIM
C_Q_W = SWA_Q_HEADS * HEAD_DIM
C_KV_W = SWA_KV_HEADS * HEAD_DIM
MIX_W = A_W + B_W + C_Q_W
IN_W = 3 * A_W + 3 * B_W + C_Q_W + 2 * C_KV_W
COL_QA, COL_KA, COL_VA = 0, A_W // 128, 2 * A_W // 128
COL_QB = 3 * A_W // 128
COL_KB = COL_QB + B_W // 128
COL_VB = COL_KB + B_W // 128
COL_QC = COL_VB + B_W // 128
COL_KC = COL_QC + C_Q_W // 128
COL_VC = COL_KC + C_KV_W // 128
MIX_COL_C, MIX_COL_A, MIX_COL_B = 0, C_Q_W // 128, (C_Q_W + A_W) // 128

V7X_VMEM_BYTES = 64 * 1024 * 1024
VMEM_LIMIT = 56 * 1024 * 1024

FF_TILE = 512
D_FF_PAD = -(-D_FF // 1024) * 1024

NT_DIMS = (((1,), (1,)), ((), ()))


def _cparams(sem):
    return pltpu.CompilerParams(dimension_semantics=sem, vmem_limit_bytes=VMEM_LIMIT)


def _cast_kernel(w_ref, o_ref, *, rows, cols, masked):
    x = w_ref[0]
    if masked:
        tr, tc = x.shape
        r = pl.program_id(1) * tr + lax.broadcasted_iota(jnp.int32, x.shape, 0)
        c = pl.program_id(2) * tc + lax.broadcasted_iota(jnp.int32, x.shape, 1)
        x = jnp.where(jnp.logical_and(r < rows, c < cols), x, 0.0)
    o_ref[0] = x.astype(o_ref.dtype)


def cast_weights(w, rows_out=None, cols_out=None, row_rotate=0):
    nl, rows, cols = w.shape
    rows_out = rows if rows_out is None else rows_out
    cols_out = cols if cols_out is None else cols_out
    tr = math.gcd(rows_out, 512)
    tc = math.gcd(cols_out, 1024)
    nrb = rows_out // tr
    assert row_rotate % tr == 0 and (row_rotate == 0 or rows_out == rows)
    shift = row_rotate // tr
    spec = pl.BlockSpec((1, tr, tc), lambda l, i, j: (l, i, j))
    return pl.pallas_call(
        functools.partial(_cast_kernel, rows=rows, cols=cols,
                          masked=(rows_out != rows or cols_out != cols)),
        grid=(nl, nrb, cols_out // tc),
        in_specs=[pl.BlockSpec((1, tr, tc), lambda l, i, j: (l, (i + shift) % nrb, j))],
        out_specs=spec,
        out_shape=jax.ShapeDtypeStruct((nl, rows_out, cols_out), BF16),
        compiler_params=_cparams(("parallel", "parallel", "parallel")),
        name="cast_weights",
    )(w)


def _gateup_kernel(x_ref, wg_ref, wu_ref, o_ref):
    x = x_ref[...]
    g = jnp.dot(x, wg_ref[...], preferred_element_type=F32)
    u = jnp.dot(x, wu_ref[...], preferred_element_type=F32)
    o_ref[...] = (g * jax.nn.sigmoid(g) * u).astype(o_ref.dtype)


def gateup(x, wg, wu, li):
    m, k = x.shape
    f = wg.shape[2]
    tm = min(1024, m)
    tn = min(FF_TILE, f)
    return pl.pallas_call(
        _gateup_kernel,
        grid=(m // tm, f // tn),
        in_specs=[
            pl.BlockSpec((tm, k), lambda i, j: (i, 0)),
            pl.BlockSpec((None, k, tn), lambda i, j: (li, 0, j)),
            pl.BlockSpec((None, k, tn), lambda i, j: (li, 0, j)),
        ],
        out_specs=pl.BlockSpec((tm, tn), lambda i, j: (i, j)),
        out_shape=jax.ShapeDtypeStruct((m, f), BF16),
        compiler_params=_cparams(("parallel", "arbitrary")),
        name="gateup",
    )(x, wg, wu)


def _matmul_kernel(*refs, nk, res_scale, acc_scale):
    if res_scale is None:
        (a_ref, w_ref, o_ref, acc_ref), res_ref = refs, None
    else:
        a_ref, w_ref, res_ref, o_ref, acc_ref = refs
    kk = pl.program_id(2)
    part = jnp.dot(a_ref[...], w_ref[...], preferred_element_type=F32)

    def finish(total):
        if res_ref is not None:
            total = res_scale * res_ref[...] + acc_scale * total
        o_ref[...] = total.astype(o_ref.dtype)

    if nk == 1:
        finish(part)
        return

    @pl.when(kk == 0)
    def _():
        acc_ref[...] = part

    @pl.when(jnp.logical_and(kk > 0, kk < nk - 1))
    def _():
        acc_ref[...] += part

    @pl.when(kk == nk - 1)
    def _():
        finish(acc_ref[...] + part)


def matmul(a, w, li, out_dtype, tk=None, res=None, res_scale=None, acc_scale=1.0):
    m, k = a.shape
    n = w.shape[2]
    tm = min(1024, m)
    tn = min(1024, n)
    tk = k if tk is None else tk
    nk = k // tk
    in_specs = [
        pl.BlockSpec((tm, tk), lambda i, j, kk: (i, kk)),
        pl.BlockSpec((None, tk, tn), lambda i, j, kk: (li, kk, j)),
    ]
    operands = [a, w]
    if res is not None:
        in_specs.append(pl.BlockSpec((tm, tn), lambda i, j, kk: (i, j)))
        operands.append(res)
    return pl.pallas_call(
        functools.partial(_matmul_kernel, nk=nk, res_scale=res_scale if res is not None else None,
                          acc_scale=acc_scale),
        grid=(m // tm, n // tn, nk),
        in_specs=in_specs,
        out_specs=pl.BlockSpec((tm, tn), lambda i, j, kk: (i, j)),
        out_shape=jax.ShapeDtypeStruct((m, n), out_dtype),
        scratch_shapes=[pltpu.VMEM((tm, tn) if nk > 1 else (8, 128), F32)],
        compiler_params=_cparams(("parallel", "parallel", "arbitrary")),
        name="matmul",
    )(*operands)


def _ple_kernel(x_ref, xb_ref, p_ref, wg_ref, wp_ref, o_ref):
    gate = jax.nn.sigmoid(jnp.dot(xb_ref[...], wg_ref[...], preferred_element_type=F32))
    proj = jnp.dot(p_ref[...], wp_ref[...], preferred_element_type=F32)
    o_ref[...] = DEEPNORM_ALPHA * x_ref[...] + gate * proj


def ple_residual(x, xb, p, wg, wp, li):
    m, d = x.shape
    pd = p.shape[2]
    tm = min(512, m)
    tn = min(1024, d)
    return pl.pallas_call(
        _ple_kernel,
        grid=(m // tm, d // tn),
        in_specs=[
            pl.BlockSpec((tm, tn), lambda i, j: (i, j)),
            pl.BlockSpec((tm, d), lambda i, j: (i, 0)),
            pl.BlockSpec((None, tm, pd), lambda i, j: (li, i, 0)),
            pl.BlockSpec((None, d, tn), lambda i, j: (li, 0, j)),
            pl.BlockSpec((None, pd, tn), lambda i, j: (li, 0, j)),
        ],
        out_specs=pl.BlockSpec((tm, tn), lambda i, j: (i, j)),
        out_shape=jax.ShapeDtypeStruct((m, d), F32),
        compiler_params=_cparams(("parallel", "arbitrary")),
        name="ple_residual",
    )(x, xb, p, wg, wp)


def _layer_norm(y, g, b):
    mu = jnp.mean(y, axis=-1, keepdims=True)
    yc = y - mu
    var = jnp.mean(yc * yc, axis=-1, keepdims=True)
    return yc * lax.rsqrt(var + LN_EPS) * g + b


def _ln_kernel(y_ref, g_ref, b_ref, of_ref, ob_ref):
    out = _layer_norm(y_ref[...], g_ref[...], b_ref[...])
    of_ref[...] = out
    ob_ref[...] = out.astype(BF16)


LN_ROWS = 256


def layernorm(y, g, b):
    m, d = y.shape
    tm = min(LN_ROWS, m)
    row = pl.BlockSpec((tm, d), lambda i: (i, 0))
    vec = pl.BlockSpec((1, d), lambda i: (0, 0))
    return pl.pallas_call(
        _ln_kernel,
        grid=(m // tm,),
        in_specs=[row, vec, vec],
        out_specs=[row, row],
        out_shape=[jax.ShapeDtypeStruct((m, d), F32), jax.ShapeDtypeStruct((m, d), BF16)],
        compiler_params=_cparams(("parallel",)),
        name="layernorm",
    )(y, g.reshape(1, d), b.reshape(1, d))


def _ln_split_kernel(y_ref, g_ref, b_ref, o0_ref, o1_ref, *, n0):
    i = pl.program_id(0)
    out = _layer_norm(y_ref[...], g_ref[...], b_ref[...])

    @pl.when(i < n0)
    def _():
        o0_ref[...] = out

    @pl.when(i >= n0)
    def _():
        o1_ref[...] = out


def layernorm_split(y, g, b, m0):
    m, d = y.shape
    tm = min(LN_ROWS, m0, m - m0)
    assert m0 % tm == 0 and (m - m0) % tm == 0
    n0 = m0 // tm
    vec = pl.BlockSpec((1, d), lambda i: (0, 0))
    return pl.pallas_call(
        functools.partial(_ln_split_kernel, n0=n0),
        grid=(m // tm,),
        in_specs=[pl.BlockSpec((tm, d), lambda i: (i, 0)), vec, vec],
        out_specs=[pl.BlockSpec((tm, d), lambda i: (jnp.minimum(i, n0 - 1), 0)),
                   pl.BlockSpec((tm, d), lambda i: (jnp.maximum(i - n0, 0), 0))],
        out_shape=[jax.ShapeDtypeStruct((m0, d), F32), jax.ShapeDtypeStruct((m - m0, d), F32)],
        compiler_params=_cparams(("arbitrary",)),
        name="layernorm_split",
    )(y, g.reshape(1, d), b.reshape(1, d))


def _merge_kernel(x0_ref, x1_ref, of_ref, ob_ref, *, n0):
    x = jnp.where(pl.program_id(0) < n0, x0_ref[...], x1_ref[...])
    of_ref[...] = x
    ob_ref[...] = x.astype(BF16)


def merge_streams(x0, x1):
    m0, d = x0.shape
    m1 = x1.shape[0]
    tm = min(LN_ROWS, m0, m1)
    assert m0 % tm == 0 and m1 % tm == 0
    n0 = m0 // tm
    row = pl.BlockSpec((tm, d), lambda i: (i, 0))
    return pl.pallas_call(
        functools.partial(_merge_kernel, n0=n0),
        grid=((m0 + m1) // tm,),
        in_specs=[pl.BlockSpec((tm, d), lambda i: (jnp.minimum(i, n0 - 1), 0)),
                  pl.BlockSpec((tm, d), lambda i: (jnp.maximum(i - n0, 0), 0))],
        out_specs=[row, row],
        out_shape=[jax.ShapeDtypeStruct((m0 + m1, d), F32), jax.ShapeDtypeStruct((m0 + m1, d), BF16)],
        compiler_params=_cparams(("parallel",)),
        name="merge_streams",
    )(x0, x1)


def _t5_bucket(rel):
    half = REL_BUCKETS // 2
    max_exact = half // 2
    n = jnp.abs(rel)
    side = jnp.where(rel > 0, half, 0)
    nf = jnp.maximum(n, 1).astype(F32)
    large = max_exact + (jnp.log(nf / max_exact) / math.log(REL_MAX_DIST / max_exact)
                         * (half - max_exact)).astype(jnp.int32)
    large = jnp.minimum(large, half - 1)
    return side + jnp.where(n < max_exact, n, large)


DIFF_TILE = 256
DIFF_ONES_ROWS = 16
SWA_TQ = 512
LOG2E = math.log2(math.e)


def _bucket_select(bucket, table_ref, col, init):
    out = jnp.full(bucket.shape, init, F32)
    for b in range(REL_BUCKETS):
        out = jnp.where(bucket == b, table_ref[b, col], out)
    return out


NA_QROWS = 8
NA_QBLK = NA_QROWS * GRID_W
NA_KBLK = NA_QBLK // 2
NA_RPB_C = 2 * NA_WIN_C - 1


def _na_build_bias(rpb_ref, tab_ref, hd):
    shape = (GRID_W, 2 * GRID_W)
    lane = lax.broadcasted_iota(jnp.int32, shape, 1)
    qc = lax.broadcasted_iota(jnp.int32, shape, 0)
    second = lane >= GRID_W
    kc = jnp.where(second, lane - GRID_W, lane)
    d = kc - qc + (NA_WIN_C - 1)
    c0 = jnp.clip(qc - NA_WIN_C // 2, 0, GRID_W - NA_WIN_C)
    col_ok = jnp.logical_and(kc >= c0, kc < c0 + NA_WIN_C)
    pairs = []
    for r in range(2 * NA_WIN_R - 2):
        t2 = jnp.full(shape, NEG_INF, F32)
        for dd in range(NA_RPB_C):
            val = jnp.where(second, rpb_ref[hd, (r + 1) * NA_RPB_C + dd], rpb_ref[hd, r * NA_RPB_C + dd])
            t2 = jnp.where(d == dd, val, t2)
        pairs.append(jnp.where(col_ok, t2, NEG_INF))
    for a0 in range(NA_WIN_R):
        for jj in range(0, NA_WIN_R, 2):
            tab_ref[hd, a0, :, jj * GRID_W:(jj + 2) * GRID_W] = pairs[a0 + jj]


def _na_kernel(kidx_ref, var_ref, q_ref, k0, k1, k2, k3, v0, v1, v2, v3, rpb_ref, mix_ref, o_ref, tab_ref):
    del mix_ref
    n = pl.program_id(0)
    hd = pl.program_id(1)
    scale = HEAD_DIM ** -0.5

    pl.when(n == 0)(functools.partial(_na_build_bias, rpb_ref, tab_ref, hd))

    q = q_ref[...]
    kspan = jnp.concatenate([k0[...], k1[...], k2[...], k3[...]], axis=0)
    vspan = jnp.concatenate([v0[...], v1[...], v2[...], v3[...]], axis=0)
    half = NA_WIN_R // 2
    span = NA_WIN_R * GRID_W

    def run(variant):
        for il in range(NA_QROWS):
            back = (min(il, half), half, max(il, half))[variant]
            start = il + half - back
            a0 = NA_WIN_R - 1 - back
            kw = kspan[start * GRID_W: start * GRID_W + span]
            vw = vspan[start * GRID_W: start * GRID_W + span]
            s = lax.dot_general(q[il * GRID_W:(il + 1) * GRID_W], kw, NT_DIMS,
                                preferred_element_type=F32) * scale + tab_ref[hd, a0]
            m = jnp.max(s, axis=-1, keepdims=True)
            p = jnp.exp(s - m)
            l = jnp.sum(p, axis=-1, keepdims=True)
            o = jnp.dot(p.astype(BF16), vw, preferred_element_type=F32) / l
            o_ref[il * GRID_W:(il + 1) * GRID_W, :] = o.astype(o_ref.dtype)

    for variant in range(3):
        pl.when(var_ref[n] == variant)(functools.partial(run, variant))


def neighbourhood_attention(h, mix, rpb, seq_lens):
    m = h.shape[0]
    nblk = m // NA_QBLK
    kidx = np.zeros((4, nblk), np.int32)
    variant = np.ones((nblk,), np.int32)
    base = 0
    for t in seq_lens:
        nb = t // NA_QBLK
        assert nb >= 2 and t % NA_QBLK == 0
        lo, hi = 2 * base, 2 * (base + nb) - 1
        for n in range(nb):
            for j in range(4):
                kidx[j, base + n] = np.clip(2 * (base + n) - 1 + j, lo, hi)
        variant[base] = 0
        variant[base + nb - 1] = 2
        base += nb

    def kmap(j, col):
        return lambda n, hd, kidx_ref, var_ref: (kidx_ref[j, n], col + hd)

    kv_specs = [pl.BlockSpec((NA_KBLK, HEAD_DIM), kmap(j, COL_KA)) for j in range(4)]
    kv_specs += [pl.BlockSpec((NA_KBLK, HEAD_DIM), kmap(j, COL_VA)) for j in range(4)]
    grid_spec = pltpu.PrefetchScalarGridSpec(
        num_scalar_prefetch=2,
        grid=(nblk, NA_HEADS),
        in_specs=[pl.BlockSpec((NA_QBLK, HEAD_DIM), lambda n, hd, *_: (n, COL_QA + hd))]
        + kv_specs
        + [pl.BlockSpec(memory_space=pltpu.SMEM), pl.BlockSpec(memory_space=pl.ANY)],
        out_specs=pl.BlockSpec((NA_QBLK, HEAD_DIM), lambda n, hd, *_: (n, MIX_COL_A + hd)),
        scratch_shapes=[pltpu.VMEM((NA_HEADS, NA_WIN_R, GRID_W, NA_WIN_R * GRID_W), F32)],
    )
    return pl.pallas_call(
        _na_kernel,
        grid_spec=grid_spec,
        out_shape=jax.ShapeDtypeStruct(mix.shape, mix.dtype),
        input_output_aliases={12: 0},
        compiler_params=_cparams(("arbitrary", "arbitrary")),
        name="neighbourhood_attention",
    )(jnp.asarray(kidx), jnp.asarray(variant), h, *([h] * 8),
      rpb.reshape(NA_HEADS, -1).astype(F32), mix)


def _diff_kernel(far_ref, rel_ref, q_ref, k_ref, v_ref, bkt_ref, lamv_ref, g_ref, mix_ref, o_ref,
                 s0_ref, s1_ref, p0_ref, p1_ref, vt_ref, bias_ref, *, t, nk, lambda_init):
    del mix_ref
    hd = pl.program_id(1)
    i = pl.program_id(2)

    @pl.when(i == 0)
    def _prepare_head():
        for o in range(3):
            bias_ref[o] = _bucket_select(bkt_ref[o], rel_ref, hd, 0.0) * LOG2E
        for c in range(nk):
            vt_ref[:HEAD_DIM, c * t:(c + 1) * t] = v_ref[c * t:(c + 1) * t, :].astype(F32).T.astype(BF16)
        vt_ref[HEAD_DIM:, :] = jnp.ones((DIFF_ONES_ROWS, nk * t), BF16)

    q = q_ref[...]
    lane = lax.broadcasted_iota(jnp.int32, q.shape, 1)
    qk_scale = jnp.asarray(DIFF_QK_DIM ** -0.5, BF16)
    zero = jnp.zeros_like(q)
    qs = (jnp.where(lane < DIFF_QK_DIM, q, zero) * qk_scale,
          jnp.where(lane >= DIFF_QK_DIM, q, zero) * qk_scale)

    c_lo = rel_ref[far_ref[0], hd] * LOG2E
    c_hi = rel_ref[far_ref[1], hd] * LOG2E

    def far_bias(j, near):
        return jnp.where(j < i - 1, c_lo, jnp.where(j > i + 1, c_hi, near))

    s_refs, p_refs = (s0_ref, s1_ref), (p0_ref, p1_ref)
    m = []
    for c in range(2):
        s = lax.dot_general(k_ref[...], qs[c], NT_DIMS, preferred_element_type=F32) * LOG2E
        s_refs[c][...] = s
        mc = jnp.full((1, t), NEG_INF, F32)
        for jc in range(nk):
            mc = jnp.maximum(mc, jnp.max(s[jc * t:(jc + 1) * t], axis=0, keepdims=True) + far_bias(jc, NEG_INF))
        m.append(mc)

    for o in (-1, 0, 1):
        j = i + o
        ok = jnp.logical_and(j >= 0, j < nk)
        row = pl.multiple_of(jnp.clip(j, 0, nk - 1) * t, t)
        tile = jnp.where(ok, bias_ref[o + 1], 0.0)
        for c in range(2):
            blk = s_refs[c][pl.ds(row, t), :] + tile
            s_refs[c][pl.ds(row, t), :] = blk
            m[c] = jnp.maximum(m[c], jnp.max(blk, axis=0, keepdims=True) + jnp.where(ok, 0.0, NEG_INF))

    for jc in range(nk):
        sh = far_bias(jc, 0.0)
        for c in range(2):
            p = jnp.exp2(s_refs[c][jc * t:(jc + 1) * t, :] - (m[c] - sh))
            p_refs[c][jc * t:(jc + 1) * t, :] = p.astype(BF16)

    pv = [jnp.dot(vt_ref[...], p_refs[c][...], preferred_element_type=F32) for c in range(2)]
    acc = [x[:HEAD_DIM] for x in pv]
    l = [x[HEAD_DIM:HEAD_DIM + 1] for x in pv]
    lamv = lamv_ref[...]
    lam = (jnp.exp(jnp.sum(lamv[0:1] * lamv[1:2], axis=-1, keepdims=True))
           - jnp.exp(jnp.sum(lamv[2:3] * lamv[3:4], axis=-1, keepdims=True)) + lambda_init)
    o_t = acc[0] / l[0] - lam * (acc[1] / l[1])
    ms = jnp.mean(o_t * o_t, axis=0, keepdims=True)
    o_t = o_t * lax.rsqrt(ms + RMS_EPS) * g_ref[...] * (1.0 - lambda_init)
    o_ref[...] = o_t.T.astype(o_ref.dtype)


def differential_attention(h, mix, row0, nb, t_seq, rel_bias, lamv, sub_g, lambda_init):
    t = DIFF_TILE
    nq = t_seq // t
    assert row0 % t_seq == 0 and t_seq % t == 0
    qb0, sb0 = row0 // t, row0 // t_seq
    kq = np.arange(t)[:, None] - np.arange(t)[None, :]
    bkt = _t5_bucket(jnp.asarray(np.stack([kq + o * t for o in (-1, 0, 1)]), dtype=jnp.int32))
    far = _t5_bucket(jnp.asarray([-2 * t, 2 * t], dtype=jnp.int32))
    smem = pl.BlockSpec(memory_space=pltpu.SMEM)
    return pl.pallas_call(
        functools.partial(_diff_kernel, t=t, nk=nq, lambda_init=lambda_init),
        grid=(nb, DIFF_HEADS, nq),
        in_specs=[
            smem, smem,
            pl.BlockSpec((t, HEAD_DIM), lambda b, hd, i: (qb0 + b * nq + i, COL_QB + hd)),
            pl.BlockSpec((t_seq, HEAD_DIM), lambda b, hd, i: (sb0 + b, COL_KB + hd)),
            pl.BlockSpec((t_seq, HEAD_DIM), lambda b, hd, i: (sb0 + b, COL_VB + hd)),
            pl.BlockSpec((3, t, t), lambda b, hd, i: (0, 0, 0)),
            pl.BlockSpec((4, DIFF_QK_DIM), lambda b, hd, i: (0, 0)),
            pl.BlockSpec((HEAD_DIM, 1), lambda b, hd, i: (0, 0)),
            pl.BlockSpec(memory_space=pl.ANY),
        ],
        out_specs=pl.BlockSpec((t, HEAD_DIM), lambda b, hd, i: (qb0 + b * nq + i, MIX_COL_B + hd)),
        scratch_shapes=[
            pltpu.VMEM((t_seq, t), F32), pltpu.VMEM((t_seq, t), F32),
            pltpu.VMEM((t_seq, t), BF16), pltpu.VMEM((t_seq, t), BF16),
            pltpu.VMEM((HEAD_DIM + DIFF_ONES_ROWS, t_seq), BF16),
            pltpu.VMEM((3, t, t), F32),
        ],
        out_shape=jax.ShapeDtypeStruct(mix.shape, mix.dtype),
        input_output_aliases={8: 0},
        compiler_params=_cparams(("arbitrary", "arbitrary", "arbitrary")),
        name="differential_attention",
    )(far, rel_bias.astype(F32), h, h, h, bkt, lamv, sub_g.reshape(HEAD_DIM, 1).astype(F32), mix)


def _swa_kernel(edge_ref, q0, q1, q2, kp, kc, kn, vp, vc, vn, bkt_ref, rel_ref, sink_ref, mix_ref, o_ref,
                bias_ref):
    del mix_ref
    i = pl.program_id(0)
    j = pl.program_id(1)
    scale = HEAD_DIM ** -0.5
    w = SWA_TQ + 2 * SWA_WINDOW

    @pl.when(i == 0)
    def _build_bias():
        for g in range(SWA_GROUP):
            head = j * SWA_GROUP + g
            bias_ref[head] = _bucket_select(bkt_ref[...], rel_ref, DIFF_HEADS + head, NEG_INF)

    kwin = jnp.concatenate([kp[...], kc[...], kn[...]], axis=0)
    vwin = jnp.concatenate([vp[...], vc[...], vn[...]], axis=0)
    col = lax.broadcasted_iota(jnp.int32, (1, w), 1)
    is_first = edge_ref[0, i] == 1
    is_last = edge_ref[1, i] == 1
    valid = jnp.logical_and(jnp.logical_or(col >= SWA_WINDOW, jnp.logical_not(is_first)),
                            jnp.logical_or(col < SWA_TQ + SWA_WINDOW, jnp.logical_not(is_last)))
    for g, q_ref in enumerate((q0, q1, q2)):
        head = j * SWA_GROUP + g
        sink = sink_ref[head]
        s = lax.dot_general(q_ref[...], kwin, NT_DIMS, preferred_element_type=F32) * scale + bias_ref[head]
        s = jnp.where(valid, s, NEG_INF)
        m = jnp.maximum(jnp.max(s, axis=-1, keepdims=True), sink)
        p = jnp.exp(s - m)
        l = jnp.sum(p, axis=-1, keepdims=True) + jnp.exp(sink - m)
        o = jnp.dot(p.astype(BF16), vwin, preferred_element_type=F32) / l
        o_ref[:, g * HEAD_DIM:(g + 1) * HEAD_DIM] = o.astype(o_ref.dtype)


def sliding_window_attention(h, mix, rel_bias, sink, seq_lens):
    m = h.shape[0]
    nblk = m // SWA_TQ
    ratio = SWA_TQ // SWA_WINDOW
    edge = np.zeros((2, nblk), np.int32)
    base = 0
    for t in seq_lens:
        assert t % SWA_TQ == 0
        edge[0, base] = 1
        base += t // SWA_TQ
        edge[1, base - 1] = 1
    last_halo = m // SWA_WINDOW - 1
    w = SWA_TQ + 2 * SWA_WINDOW
    rel = (np.arange(w)[None, :] - SWA_WINDOW) - np.arange(SWA_TQ)[:, None]
    bkt = jnp.where(jnp.asarray(np.abs(rel) <= SWA_WINDOW), _t5_bucket(jnp.asarray(rel, dtype=jnp.int32)), -1)

    def qmap(g):
        return lambda i, j, *_: (i, COL_QC + j * SWA_GROUP + g)

    def halo(col):
        return [
            pl.BlockSpec((SWA_WINDOW, HEAD_DIM), lambda i, j, *_: (jnp.maximum(ratio * i - 1, 0), col + j)),
            pl.BlockSpec((SWA_TQ, HEAD_DIM), lambda i, j, *_: (i, col + j)),
            pl.BlockSpec((SWA_WINDOW, HEAD_DIM),
                         lambda i, j, *_: (jnp.minimum(ratio * (i + 1), last_halo), col + j)),
        ]

    smem = pl.BlockSpec(memory_space=pltpu.SMEM)
    grid_spec = pltpu.PrefetchScalarGridSpec(
        num_scalar_prefetch=1,
        grid=(nblk, SWA_KV_HEADS),
        in_specs=[pl.BlockSpec((SWA_TQ, HEAD_DIM), qmap(g)) for g in range(SWA_GROUP)]
        + halo(COL_KC) + halo(COL_VC)
        + [pl.BlockSpec((SWA_TQ, w), lambda i, j, *_: (0, 0)), smem, smem,
           pl.BlockSpec(memory_space=pl.ANY)],
        out_specs=pl.BlockSpec((SWA_TQ, SWA_GROUP * HEAD_DIM),
                               lambda i, j, *_: (i, MIX_COL_C // SWA_GROUP + j)),
        scratch_shapes=[pltpu.VMEM((SWA_Q_HEADS, SWA_TQ, w), F32)],
    )
    assert MIX_COL_C % SWA_GROUP == 0
    return pl.pallas_call(
        _swa_kernel,
        grid_spec=grid_spec,
        out_shape=jax.ShapeDtypeStruct(mix.shape, mix.dtype),
        input_output_aliases={13: 0},
        compiler_params=_cparams(("arbitrary", "arbitrary")),
        name="sliding_window_attention",
    )(jnp.asarray(edge), *([h] * 9), bkt, rel_bias.astype(F32), sink.astype(F32), mix)


def encoder_trunk(x0, x1, p, seq_groups, w_in, w_out, na_rpb, lam_q1, lam_k1, lam_q2, lam_k2,
                  diff_g, sink, rel_bias, ffn1_gate, ffn1_up, ffn1_down,
                  ffn2_gate, ffn2_up, ffn2_down, ple_gate, ple_proj, ln_g, ln_b):
    x, xb = merge_streams(x0, x1)
    m = x.shape[0]
    seq_lens = [t for _, nb, t in seq_groups for _ in range(nb)]
    pb = p.astype(BF16)
    w_in, ple_gate, ple_proj = (cast_weights(w) for w in (w_in, ple_gate, ple_proj))
    w_out = cast_weights(w_out, row_rotate=A_W + B_W)
    ffn1_gate, ffn1_up, ffn2_gate, ffn2_up = (
        cast_weights(w, cols_out=D_FF_PAD) for w in (ffn1_gate, ffn1_up, ffn2_gate, ffn2_up))
    ffn1_down, ffn2_down = (cast_weights(w, rows_out=D_FF_PAD) for w in (ffn1_down, ffn2_down))
    ffn_tk = D_FF_PAD // 4
    for li in range(DEPTH):
        hid = gateup(xb, ffn1_gate, ffn1_up, li)
        y = matmul(hid, ffn1_down, li, F32, tk=ffn_tk, res=x, res_scale=DEEPNORM_ALPHA, acc_scale=0.5)
        x, xb = layernorm(y, ln_g[li, 0], ln_b[li, 0])
        h = matmul(xb, w_in, li, BF16)
        lambda_init = 0.8 - 0.6 * math.exp(-0.3 * li)
        lamv = jnp.stack([lam_q1[li], lam_k1[li], lam_q2[li], lam_k2[li]]).astype(F32)
        mix = jnp.zeros((m, MIX_W), BF16)
        mix = neighbourhood_attention(h, mix, na_rpb[li], seq_lens)
        for row0, nb, t in seq_groups:
            mix = differential_attention(h, mix, row0, nb, t, rel_bias, lamv, diff_g[li], lambda_init)
        mix = sliding_window_attention(h, mix, rel_bias, sink[li], seq_lens)
        y = matmul(mix, w_out, li, F32, res=x, res_scale=DEEPNORM_ALPHA)
        x, xb = layernorm(y, ln_g[li, 1], ln_b[li, 1])
        res = ple_residual(x, xb, pb, ple_gate, ple_proj, li)
        hid = gateup(xb, ffn2_gate, ffn2_up, li)
        y = matmul(hid, ffn2_down, li, F32, tk=ffn_tk, res=res, res_scale=1.0, acc_scale=0.5)
        if li < DEPTH - 1:
            x, xb = layernorm(y, ln_g[li, 2], ln_b[li, 2])
    return layernorm_split(y, ln_g[DEPTH - 1, 2], ln_b[DEPTH - 1, 2], x0.shape[0])


def kernel(x_prompt, x_sample, p_prompt, p_sample, w_in, w_out, na_rpb, lam_q1, lam_k1, lam_q2, lam_k2,
           diff_g, sink, rel_bias, ffn1_gate, ffn1_up, ffn1_down, ffn2_gate, ffn2_up, ffn2_down,
           ple_gate, ple_proj, ln_g, ln_b):
    bp, tp, d = x_prompt.shape
    bs, ts, _ = x_sample.shape
    mp, ms = bp * tp, bs * ts
    p = jnp.concatenate([p_prompt.reshape(DEPTH, mp, PLE_DIM), p_sample.reshape(DEPTH, ms, PLE_DIM)], axis=1)
    seq_groups = [(0, bp, tp), (mp, bs, ts)]
    y0, y1 = encoder_trunk(x_prompt.reshape(mp, d), x_sample.reshape(ms, d), p, seq_groups,
                           w_in, w_out, na_rpb, lam_q1, lam_k1, lam_q2, lam_k2,
                           diff_g, sink, rel_bias, ffn1_gate, ffn1_up, ffn1_down,
                           ffn2_gate, ffn2_up, ffn2_down, ple_gate, ple_proj, ln_g, ln_b)
    return y0.reshape(bp, tp, d), y1.reshape(bs, ts, d)
```

```python
import functools
import math

import jax
import jax.numpy as jnp
import numpy as np
from jax import lax
from jax.experimental import pallas as pl
from jax.experimental.pallas import tpu as pltpu

F32 = jnp.float32
BF16 = jnp.bfloat16

D_MODEL = 4096
DEPTH = 2
GRID_W = 64
HEAD_DIM = 128
NA_HEADS = 8
NA_WIN_R = 8
NA_WIN_C = 16
DIFF_HEADS = 12
DIFF_QK_DIM = 64
SWA_Q_HEADS = 12
SWA_KV_HEADS = 4
SWA_GROUP = SWA_Q_HEADS // SWA_KV_HEADS
SWA_WINDOW = 128
REL_BUCKETS = 32
REL_MAX_DIST = 128
D_FF = 11008
PLE_DIM = 256
LN_EPS = 1e-5
RMS_EPS = 1e-5
NEG_INF = -1e30
DEEPNORM_ALPHA = (2 * DEPTH) ** 0.25

A_W = NA_HEADS * HEAD_DIM
B_W = DIFF_HEADS * HEAD_DIM
C_Q_W = SWA_Q_HEADS * HEAD_DIM
C_KV_W = SWA_KV_HEADS * HEAD_DIM
MIX_W = A_W + B_W + C_Q_W
IN_W = 3 * A_W + 3 * B_W + C_Q_W + 2 * C_KV_W
COL_QA, COL_KA, COL_VA = 0, A_W // 128, 2 * A_W // 128
COL_QB = 3 * A_W // 128
COL_KB = COL_QB + B_W // 128
COL_VB = COL_KB + B_W // 128
COL_QC = COL_VB + B_W // 128
COL_KC = COL_QC + C_Q_W // 128
COL_VC = COL_KC + C_KV_W // 128
MIX_COL_C, MIX_COL_A, MIX_COL_B = 0, C_Q_W // 128, (C_Q_W + A_W) // 128

V7X_VMEM_BYTES = 64 * 1024 * 1024
VMEM_LIMIT = 56 * 1024 * 1024

FF_TILE = 512
D_FF_PAD = -(-D_FF // 1024) * 1024

NT_DIMS = (((1,), (1,)), ((), ()))


def _cparams(sem):
    return pltpu.CompilerParams(dimension_semantics=sem, vmem_limit_bytes=VMEM_LIMIT)


def _cast_kernel(w_ref, o_ref, *, rows, cols, masked):
    x = w_ref[0]
    if masked:
        tr, tc = x.shape
        r = pl.program_id(1) * tr + lax.broadcasted_iota(jnp.int32, x.shape, 0)
        c = pl.program_id(2) * tc + lax.broadcasted_iota(jnp.int32, x.shape, 1)
        x = jnp.where(jnp.logical_and(r < rows, c < cols), x, 0.0)
    o_ref[0] = x.astype(o_ref.dtype)


def cast_weights(w, rows_out=None, cols_out=None, row_rotate=0):
    nl, rows, cols = w.shape
    rows_out = rows if rows_out is None else rows_out
    cols_out = cols if cols_out is None else cols_out
    tr = math.gcd(math.gcd(rows_out, 1024), row_rotate)
    tc = math.gcd(cols_out, 2048)
    nrb = rows_out // tr
    assert row_rotate == 0 or rows_out == rows
    shift = row_rotate // tr
    spec = pl.BlockSpec((1, tr, tc), lambda l, i, j: (l, i, j))
    return pl.pallas_call(
        functools.partial(_cast_kernel, rows=rows, cols=cols,
                          masked=(rows_out != rows or cols_out != cols)),
        grid=(nl, nrb, cols_out // tc),
        in_specs=[pl.BlockSpec((1, tr, tc), lambda l, i, j: (l, (i + shift) % nrb, j))],
        out_specs=spec,
        out_shape=jax.ShapeDtypeStruct((nl, rows_out, cols_out), BF16),
        compiler_params=_cparams(("parallel", "parallel", "parallel")),
        name="cast_weights",
    )(w)


def _gateup_kernel(x_ref, wg_ref, wu_ref, o_ref):
    x = x_ref[...]
    g = jnp.dot(x, wg_ref[...], preferred_element_type=F32)
    u = jnp.dot(x, wu_ref[...], preferred_element_type=F32)
    o_ref[...] = (g * jax.nn.sigmoid(g) * u).astype(o_ref.dtype)


def gateup(x, wg, wu, li):
    m, k = x.shape
    f = wg.shape[2]
    tm = min(1024, m)
    tn = min(FF_TILE, f)
    return pl.pallas_call(
        _gateup_kernel,
        grid=(m // tm, f // tn),
        in_specs=[
            pl.BlockSpec((tm, k), lambda i, j: (i, 0)),
            pl.BlockSpec((None, k, tn), lambda i, j: (li, 0, j)),
            pl.BlockSpec((None, k, tn), lambda i, j: (li, 0, j)),
        ],
        out_specs=pl.BlockSpec((tm, tn), lambda i, j: (i, j)),
        out_shape=jax.ShapeDtypeStruct((m, f), BF16),
        compiler_params=_cparams(("parallel", "arbitrary")),
        name="gateup",
    )(x, wg, wu)


def _matmul_kernel(*refs, nk, res_scale, acc_scale):
    if res_scale is None:
        (a_ref, w_ref, o_ref, acc_ref), res_ref = refs, None
    else:
        a_ref, w_ref, res_ref, o_ref, acc_ref = refs
    kk = pl.program_id(2)

    def finish(total):
        if res_ref is not None:
            total = res_scale * res_ref[...] + acc_scale * total
        o_ref[...] = total.astype(o_ref.dtype)

    if nk == 1:
        finish(jnp.dot(a_ref[...], w_ref[...], preferred_element_type=F32))
        return

    @pl.when(kk == 0)
    def _():
        acc_ref[...] = jnp.zeros_like(acc_ref)

    acc_ref[...] += jnp.dot(a_ref[...], w_ref[...], preferred_element_type=F32)

    @pl.when(kk == nk - 1)
    def _():
        finish(acc_ref[...])


def matmul(a, w, li, out_dtype, tk=None, res=None, res_scale=None, acc_scale=1.0):
    m, k = a.shape
    n = w.shape[2]
    tm = min(1024, m)
    tn = min(1024, n)
    tk = k if tk is None else tk
    nk = k // tk
    in_specs = [
        pl.BlockSpec((tm, tk), lambda i, j, kk: (i, kk)),
        pl.BlockSpec((None, tk, tn), lambda i, j, kk: (li, kk, j)),
    ]
    operands = [a, w]
    if res is not None:
        in_specs.append(pl.BlockSpec((tm, tn), lambda i, j, kk: (i, j)))
        operands.append(res)
    return pl.pallas_call(
        functools.partial(_matmul_kernel, nk=nk, res_scale=res_scale if res is not None else None,
                          acc_scale=acc_scale),
        grid=(m // tm, n // tn, nk),
        in_specs=in_specs,
        out_specs=pl.BlockSpec((tm, tn), lambda i, j, kk: (i, j)),
        out_shape=jax.ShapeDtypeStruct((m, n), out_dtype),
        scratch_shapes=[pltpu.VMEM((tm, tn) if nk > 1 else (8, 128), F32)],
        compiler_params=_cparams(("parallel", "parallel", "arbitrary")),
        name="matmul",
    )(*operands)


def _ple_kernel(x_ref, xb_ref, p_ref, wg_ref, wp_ref, o_ref):
    gate = jax.nn.sigmoid(jnp.dot(xb_ref[...], wg_ref[...], preferred_element_type=F32))
    proj = jnp.dot(p_ref[...], wp_ref[...], preferred_element_type=F32)
    o_ref[...] = DEEPNORM_ALPHA * x_ref[...] + gate * proj


def ple_residual(x, xb, p, wg, wp, li):
    m, d = x.shape
    pd = p.shape[2]
    tm = min(512, m)
    tn = min(1024, d)
    return pl.pallas_call(
        _ple_kernel,
        grid=(m // tm, d // tn),
        in_specs=[
            pl.BlockSpec((tm, tn), lambda i, j: (i, j)),
            pl.BlockSpec((tm, d), lambda i, j: (i, 0)),
            pl.BlockSpec((None, tm, pd), lambda i, j: (li, i, 0)),
            pl.BlockSpec((None, d, tn), lambda i, j: (li, 0, j)),
            pl.BlockSpec((None, pd, tn), lambda i, j: (li, 0, j)),
        ],
        out_specs=pl.BlockSpec((tm, tn), lambda i, j: (i, j)),
        out_shape=jax.ShapeDtypeStruct((m, d), F32),
        compiler_params=_cparams(("parallel", "arbitrary")),
        name="ple_residual",
    )(x, xb, p, wg, wp)


def _layer_norm(y, g, b):
    mu = jnp.mean(y, axis=-1, keepdims=True)
    yc = y - mu
    var = jnp.mean(yc * yc, axis=-1, keepdims=True)
    return yc * lax.rsqrt(var + LN_EPS) * g + b


def _ln_kernel(y_ref, g_ref, b_ref, of_ref, ob_ref):
    out = _layer_norm(y_ref[...], g_ref[...], b_ref[...])
    of_ref[...] = out
    ob_ref[...] = out.astype(BF16)


LN_ROWS = 256


def layernorm(y, g, b):
    m, d = y.shape
    tm = min(LN_ROWS, m)
    row = pl.BlockSpec((tm, d), lambda i: (i, 0))
    vec = pl.BlockSpec((1, d), lambda i: (0, 0))
    return pl.pallas_call(
        _ln_kernel,
        grid=(m // tm,),
        in_specs=[row, vec, vec],
        out_specs=[row, row],
        out_shape=[jax.ShapeDtypeStruct((m, d), F32), jax.ShapeDtypeStruct((m, d), BF16)],
        compiler_params=_cparams(("parallel",)),
        name="layernorm",
    )(y, g.reshape(1, d), b.reshape(1, d))


def _ln_split_kernel(y_ref, g_ref, b_ref, o0_ref, o1_ref, *, n0):
    i = pl.program_id(0)
    out = _layer_norm(y_ref[...], g_ref[...], b_ref[...])

    @pl.when(i < n0)
    def _():
        o0_ref[...] = out

    @pl.when(i >= n0)
    def _():
        o1_ref[...] = out


def layernorm_split(y, g, b, m0):
    m, d = y.shape
    tm = min(LN_ROWS, m0, m - m0)
    assert m0 % tm == 0 and (m - m0) % tm == 0
    n0 = m0 // tm
    vec = pl.BlockSpec((1, d), lambda i: (0, 0))
    return pl.pallas_call(
        functools.partial(_ln_split_kernel, n0=n0),
        grid=(m // tm,),
        in_specs=[pl.BlockSpec((tm, d), lambda i: (i, 0)), vec, vec],
        out_specs=[pl.BlockSpec((tm, d), lambda i: (jnp.minimum(i, n0 - 1), 0)),
                   pl.BlockSpec((tm, d), lambda i: (jnp.maximum(i - n0, 0), 0))],
        out_shape=[jax.ShapeDtypeStruct((m0, d), F32), jax.ShapeDtypeStruct((m - m0, d), F32)],
        compiler_params=_cparams(("arbitrary",)),
        name="layernorm_split",
    )(y, g.reshape(1, d), b.reshape(1, d))


def _merge_kernel(x0_ref, x1_ref, of_ref, ob_ref, *, n0):
    x = jnp.where(pl.program_id(0) < n0, x0_ref[...], x1_ref[...])
    of_ref[...] = x
    ob_ref[...] = x.astype(BF16)


def merge_streams(x0, x1):
    m0, d = x0.shape
    m1 = x1.shape[0]
    tm = min(LN_ROWS, m0, m1)
    assert m0 % tm == 0 and m1 % tm == 0
    n0 = m0 // tm
    row = pl.BlockSpec((tm, d), lambda i: (i, 0))
    return pl.pallas_call(
        functools.partial(_merge_kernel, n0=n0),
        grid=((m0 + m1) // tm,),
        in_specs=[pl.BlockSpec((tm, d), lambda i: (jnp.minimum(i, n0 - 1), 0)),
                  pl.BlockSpec((tm, d), lambda i: (jnp.maximum(i - n0, 0), 0))],
        out_specs=[row, row],
        out_shape=[jax.ShapeDtypeStruct((m0 + m1, d), F32), jax.ShapeDtypeStruct((m0 + m1, d), BF16)],
        compiler_params=_cparams(("parallel",)),
        name="merge_streams",
    )(x0, x1)


def _t5_bucket(rel):
    half = REL_BUCKETS // 2
    max_exact = half // 2
    n = jnp.abs(rel)
    side = jnp.where(rel > 0, half, 0)
    nf = jnp.maximum(n, 1).astype(F32)
    large = max_exact + (jnp.log(nf / max_exact) / math.log(REL_MAX_DIST / max_exact)
                         * (half - max_exact)).astype(jnp.int32)
    large = jnp.minimum(large, half - 1)
    return side + jnp.where(n < max_exact, n, large)


DIFF_TILE = 256
DIFF_ONES_ROWS = 16
DIFF_GROUP = 4
SWA_TQ = 512
LOG2E = math.log2(math.e)


def _dependent_zero(x):
    bits = lax.bitcast_convert_type(jnp.max(x, axis=1, keepdims=True), jnp.uint32)
    bits = lax.shift_right_logical(lax.shift_right_logical(bits, jnp.uint32(16)), jnp.uint32(16))
    return lax.bitcast_convert_type(bits, F32)


def _bucket_select(bucket, table_ref, col, init):
    out = jnp.full(bucket.shape, init, F32)
    for b in range(REL_BUCKETS):
        out = jnp.where(bucket == b, table_ref[b, col], out)
    return out


NA_QROWS = 8
NA_QBLK = NA_QROWS * GRID_W
NA_KBLK = NA_QBLK // 2
NA_RPB_C = 2 * NA_WIN_C - 1


def _na_build_bias(rpb_ref, tab_ref, hd):
    shape = (GRID_W, 2 * GRID_W)
    lane = lax.broadcasted_iota(jnp.int32, shape, 1)
    qc = lax.broadcasted_iota(jnp.int32, shape, 0)
    second = lane >= GRID_W
    kc = jnp.where(second, lane - GRID_W, lane)
    d = kc - qc + (NA_WIN_C - 1)
    c0 = jnp.clip(qc - NA_WIN_C // 2, 0, GRID_W - NA_WIN_C)
    col_ok = jnp.logical_and(kc >= c0, kc < c0 + NA_WIN_C)
    pairs = []
    for r in range(2 * NA_WIN_R - 2):
        t2 = jnp.full(shape, NEG_INF, F32)
        for dd in range(NA_RPB_C):
            val = jnp.where(second, rpb_ref[hd, (r + 1) * NA_RPB_C + dd], rpb_ref[hd, r * NA_RPB_C + dd])
            t2 = jnp.where(d == dd, val, t2)
        pairs.append(jnp.where(col_ok, t2, NEG_INF))
    for a0 in range(NA_WIN_R):
        for jj in range(0, NA_WIN_R, 2):
            tab_ref[hd, a0, :, jj * GRID_W:(jj + 2) * GRID_W] = pairs[a0 + jj]


def _na_kernel(kidx_ref, var_ref, q_ref, k0, k1, k2, k3, v0, v1, v2, v3, rpb_ref, mix_ref, o_ref, tab_ref):
    del mix_ref
    n = pl.program_id(0)
    hd = pl.program_id(1)
    scale = HEAD_DIM ** -0.5

    pl.when(n == 0)(functools.partial(_na_build_bias, rpb_ref, tab_ref, hd))

    q = q_ref[...]
    kspan = jnp.concatenate([k0[...], k1[...], k2[...], k3[...]], axis=0)
    vspan = jnp.concatenate([v0[...], v1[...], v2[...], v3[...]], axis=0)
    half = NA_WIN_R // 2
    span = NA_WIN_R * GRID_W

    def run(variant):
        for il in range(NA_QROWS):
            back = (min(il, half), half, max(il, half))[variant]
            start = il + half - back
            a0 = NA_WIN_R - 1 - back
            kw = kspan[start * GRID_W: start * GRID_W + span]
            vw = vspan[start * GRID_W: start * GRID_W + span]
            s = lax.dot_general(q[il * GRID_W:(il + 1) * GRID_W], kw, NT_DIMS,
                                preferred_element_type=F32) * scale + tab_ref[hd, a0]
            m = jnp.max(s, axis=-1, keepdims=True)
            p = jnp.exp(s - m)
            l = jnp.sum(p, axis=-1, keepdims=True)
            o = jnp.dot(p.astype(BF16), vw, preferred_element_type=F32) / l
            o_ref[il * GRID_W:(il + 1) * GRID_W, :] = o.astype(o_ref.dtype)

    for variant in range(3):
        pl.when(var_ref[n] == variant)(functools.partial(run, variant))


def neighbourhood_attention(h, mix, rpb, seq_lens):
    m = h.shape[0]
    nblk = m // NA_QBLK
    kidx = np.zeros((4, nblk), np.int32)
    variant = np.ones((nblk,), np.int32)
    base = 0
    for t in seq_lens:
        nb = t // NA_QBLK
        assert nb >= 2 and t % NA_QBLK == 0
        lo, hi = 2 * base, 2 * (base + nb) - 1
        for n in range(nb):
            for j in range(4):
                kidx[j, base + n] = np.clip(2 * (base + n) - 1 + j, lo, hi)
        variant[base] = 0
        variant[base + nb - 1] = 2
        base += nb

    def kmap(j, col):
        return lambda n, hd, kidx_ref, var_ref: (kidx_ref[j, n], col + hd)

    kv_specs = [pl.BlockSpec((NA_KBLK, HEAD_DIM), kmap(j, COL_KA)) for j in range(4)]
    kv_specs += [pl.BlockSpec((NA_KBLK, HEAD_DIM), kmap(j, COL_VA)) for j in range(4)]
    grid_spec = pltpu.PrefetchScalarGridSpec(
        num_scalar_prefetch=2,
        grid=(nblk, NA_HEADS),
        in_specs=[pl.BlockSpec((NA_QBLK, HEAD_DIM), lambda n, hd, *_: (n, COL_QA + hd))]
        + kv_specs
        + [pl.BlockSpec(memory_space=pltpu.SMEM), pl.BlockSpec(memory_space=pl.ANY)],
        out_specs=pl.BlockSpec((NA_QBLK, HEAD_DIM), lambda n, hd, *_: (n, MIX_COL_A + hd)),
        scratch_shapes=[pltpu.VMEM((NA_HEADS, NA_WIN_R, GRID_W, NA_WIN_R * GRID_W), F32)],
    )
    return pl.pallas_call(
        _na_kernel,
        grid_spec=grid_spec,
        out_shape=jax.ShapeDtypeStruct(mix.shape, mix.dtype),
        input_output_aliases={12: 0},
        compiler_params=_cparams(("arbitrary", "arbitrary")),
        name="neighbourhood_attention",
    )(jnp.asarray(kidx), jnp.asarray(variant), h, *([h] * 8),
      rpb.reshape(NA_HEADS, -1).astype(F32), mix)


def _diff_kernel(far_ref, rel_ref, q_ref, k_ref, v_ref, bkt_ref, lamv_ref, g_ref, mix_ref, o_ref,
                 s0_ref, s1_ref, p0_ref, p1_ref, vt_ref, bias_ref, *, t, nk, lambda_init):
    del mix_ref
    hd = pl.program_id(1)
    i = pl.program_id(2)

    @pl.when(i == 0)
    def _prepare_head():
        for o in range(3):
            bias_ref[o] = _bucket_select(bkt_ref[o], rel_ref, hd, 0.0) * LOG2E
        for c in range(nk):
            vt_ref[:HEAD_DIM, c * t:(c + 1) * t] = v_ref[c * t:(c + 1) * t, :].astype(F32).T.astype(BF16)
        vt_ref[HEAD_DIM:, :] = jnp.ones((DIFF_ONES_ROWS, nk * t), BF16)

    q = q_ref[...]
    lane = lax.broadcasted_iota(jnp.int32, q.shape, 1)
    qk_scale = jnp.asarray(DIFF_QK_DIM ** -0.5, BF16)
    zero = jnp.zeros_like(q)
    qs = (jnp.where(lane < DIFF_QK_DIM, q, zero) * qk_scale,
          jnp.where(lane >= DIFF_QK_DIM, q, zero) * qk_scale)

    c_lo = rel_ref[far_ref[0], hd] * LOG2E
    c_hi = rel_ref[far_ref[1], hd] * LOG2E

    def far_bias(j, near):
        return jnp.where(j < i - 1, c_lo, jnp.where(j > i + 1, c_hi, near))

    kg = DIFF_GROUP * t
    ngroups = nk // DIFF_GROUP

    def scores(c, s_ref, lo, hi, m):
        s = lax.dot_general(k_ref[lo:hi, :], qs[c], NT_DIMS, preferred_element_type=F32) * LOG2E
        s_ref[lo:hi, :] = s
        for u in range((hi - lo) // t):
            m = jnp.maximum(m, jnp.max(s[u * t:(u + 1) * t], axis=0, keepdims=True)
                            + far_bias(lo // t + u, NEG_INF))
        return m

    def add_diagonal_bias(s_ref, m):
        for o in (-1, 0, 1):
            j = i + o
            ok = jnp.logical_and(j >= 0, j < nk)
            row = pl.multiple_of(jnp.clip(j, 0, nk - 1) * t, t)
            blk = s_ref[pl.ds(row, t), :] + jnp.where(ok, bias_ref[o + 1], 0.0)
            s_ref[pl.ds(row, t), :] = blk
            m = jnp.maximum(m, jnp.max(blk, axis=0, keepdims=True) + jnp.where(ok, 0.0, NEG_INF))
        return m

    def exp_group(s_ref, p_ref, m, g):
        for jc in range(g * DIFF_GROUP, (g + 1) * DIFF_GROUP):
            p = jnp.exp2(s_ref[jc * t:(jc + 1) * t, :] - (m - far_bias(jc, 0.0)))
            p_ref[jc * t:(jc + 1) * t, :] = p.astype(BF16)

    neg = jnp.full((1, t), NEG_INF, F32)
    m0 = neg
    for g in range(ngroups):
        m0 = scores(0, s0_ref, g * kg, (g + 1) * kg, m0)
    qs = (qs[0], qs[1] + _dependent_zero(m0).astype(BF16))
    m0 = add_diagonal_bias(s0_ref, m0)
    m1 = neg
    for g in range(ngroups):
        m1 = scores(1, s1_ref, g * kg, (g + 1) * kg, m1)
        exp_group(s0_ref, p0_ref, m0, g)
    m1 = add_diagonal_bias(s1_ref, m1)
    pv0 = jnp.zeros((HEAD_DIM + DIFF_ONES_ROWS, t), F32)
    for g in range(ngroups):
        pv0 = pv0 + jnp.dot(vt_ref[:, g * kg:(g + 1) * kg], p0_ref[g * kg:(g + 1) * kg, :],
                            preferred_element_type=F32)
        exp_group(s1_ref, p1_ref, m1, g)
    half = (ngroups + 1) // 2 * kg
    pv1 = jnp.dot(vt_ref[:, :half], p1_ref[:half, :], preferred_element_type=F32)
    if half < nk * t:
        pv1 = pv1 + jnp.dot(vt_ref[:, half:], p1_ref[half:, :], preferred_element_type=F32)
    pv = [pv0, pv1]
    acc = [x[:HEAD_DIM] for x in pv]
    l = [x[HEAD_DIM:HEAD_DIM + 1] for x in pv]
    lamv = lamv_ref[...]
    lam = (jnp.exp(jnp.sum(lamv[0:1] * lamv[1:2], axis=-1, keepdims=True))
           - jnp.exp(jnp.sum(lamv[2:3] * lamv[3:4], axis=-1, keepdims=True)) + lambda_init)
    o_t = acc[0] / l[0] - lam * (acc[1] / l[1])
    ms = jnp.mean(o_t * o_t, axis=0, keepdims=True)
    o_t = o_t * lax.rsqrt(ms + RMS_EPS) * g_ref[...] * (1.0 - lambda_init)
    o_ref[...] = o_t.T.astype(o_ref.dtype)


def differential_attention(h, mix, row0, nb, t_seq, rel_bias, lamv, sub_g, lambda_init):
    t = DIFF_TILE
    nq = t_seq // t
    assert row0 % t_seq == 0 and t_seq % t == 0
    qb0, sb0 = row0 // t, row0 // t_seq
    kq = np.arange(t)[:, None] - np.arange(t)[None, :]
    bkt = _t5_bucket(jnp.asarray(np.stack([kq + o * t for o in (-1, 0, 1)]), dtype=jnp.int32))
    far = _t5_bucket(jnp.asarray([-2 * t, 2 * t], dtype=jnp.int32))
    smem = pl.BlockSpec(memory_space=pltpu.SMEM)
    return pl.pallas_call(
        functools.partial(_diff_kernel, t=t, nk=nq, lambda_init=lambda_init),
        grid=(nb, DIFF_HEADS, nq),
        in_specs=[
            smem, smem,
            pl.BlockSpec((t, HEAD_DIM), lambda b, hd, i: (qb0 + b * nq + i, COL_QB + hd)),
            pl.BlockSpec((t_seq, HEAD_DIM), lambda b, hd, i: (sb0 + b, COL_KB + hd)),
            pl.BlockSpec((t_seq, HEAD_DIM), lambda b, hd, i: (sb0 + b, COL_VB + hd)),
            pl.BlockSpec((3, t, t), lambda b, hd, i: (0, 0, 0)),
            pl.BlockSpec((4, DIFF_QK_DIM), lambda b, hd, i: (0, 0)),
            pl.BlockSpec((HEAD_DIM, 1), lambda b, hd, i: (0, 0)),
            pl.BlockSpec(memory_space=pl.ANY),
        ],
        out_specs=pl.BlockSpec((t, HEAD_DIM), lambda b, hd, i: (qb0 + b * nq + i, MIX_COL_B + hd)),
        scratch_shapes=[
            pltpu.VMEM((t_seq, t), F32), pltpu.VMEM((t_seq, t), F32),
            pltpu.VMEM((t_seq, t), BF16), pltpu.VMEM((t_seq, t), BF16),
            pltpu.VMEM((HEAD_DIM + DIFF_ONES_ROWS, t_seq), BF16),
            pltpu.VMEM((3, t, t), F32),
        ],
        out_shape=jax.ShapeDtypeStruct(mix.shape, mix.dtype),
        input_output_aliases={8: 0},
        compiler_params=_cparams(("arbitrary", "arbitrary", "arbitrary")),
        name="differential_attention",
    )(far, rel_bias.astype(F32), h, h, h, bkt, lamv, sub_g.reshape(HEAD_DIM, 1).astype(F32), mix)


def _swa_kernel(edge_ref, q0, q1, q2, kp, kc, kn, vp, vc, vn, bkt_ref, rel_ref, sink_ref, mix_ref, o_ref,
                bias_ref):
    del mix_ref
    i = pl.program_id(0)
    j = pl.program_id(1)
    scale = HEAD_DIM ** -0.5
    w = SWA_TQ + 2 * SWA_WINDOW

    @pl.when(i == 0)
    def _build_bias():
        for g in range(SWA_GROUP):
            head = j * SWA_GROUP + g
            bias_ref[head] = _bucket_select(bkt_ref[...], rel_ref, DIFF_HEADS + head, NEG_INF)

    kwin = jnp.concatenate([kp[...], kc[...], kn[...]], axis=0)
    vwin = jnp.concatenate([vp[...], vc[...], vn[...]], axis=0)
    col = lax.broadcasted_iota(jnp.int32, (1, w), 1)
    is_first = edge_ref[0, i] == 1
    is_last = edge_ref[1, i] == 1
    valid = jnp.logical_and(jnp.logical_or(col >= SWA_WINDOW, jnp.logical_not(is_first)),
                            jnp.logical_or(col < SWA_TQ + SWA_WINDOW, jnp.logical_not(is_last)))
    for g, q_ref in enumerate((q0, q1, q2)):
        head = j * SWA_GROUP + g
        sink = sink_ref[head]
        s = lax.dot_general(q_ref[...], kwin, NT_DIMS, preferred_element_type=F32) * scale + bias_ref[head]
        s = jnp.where(valid, s, NEG_INF)
        m = jnp.maximum(jnp.max(s, axis=-1, keepdims=True), sink)
        p = jnp.exp(s - m)
        l = jnp.sum(p, axis=-1, keepdims=True) + jnp.exp(sink - m)
        o = jnp.dot(p.astype(BF16), vwin, preferred_element_type=F32) / l
        o_ref[:, g * HEAD_DIM:(g + 1) * HEAD_DIM] = o.astype(o_ref.dtype)


def sliding_window_attention(h, mix, rel_bias, sink, seq_lens):
    m = h.shape[0]
    nblk = m // SWA_TQ
    ratio = SWA_TQ // SWA_WINDOW
    edge = np.zeros((2, nblk), np.int32)
    base = 0
    for t in seq_lens:
        assert t % SWA_TQ == 0
        edge[0, base] = 1
        base += t // SWA_TQ
        edge[1, base - 1] = 1
    last_halo = m // SWA_WINDOW - 1
    w = SWA_TQ + 2 * SWA_WINDOW
    rel = (np.arange(w)[None, :] - SWA_WINDOW) - np.arange(SWA_TQ)[:, None]
    bkt = jnp.where(jnp.asarray(np.abs(rel) <= SWA_WINDOW), _t5_bucket(jnp.asarray(rel, dtype=jnp.int32)), -1)

    def qmap(g):
        return lambda i, j, *_: (i, COL_QC + j * SWA_GROUP + g)

    def halo(col):
        return [
            pl.BlockSpec((SWA_WINDOW, HEAD_DIM), lambda i, j, *_: (jnp.maximum(ratio * i - 1, 0), col + j)),
            pl.BlockSpec((SWA_TQ, HEAD_DIM), lambda i, j, *_: (i, col + j)),
            pl.BlockSpec((SWA_WINDOW, HEAD_DIM),
                         lambda i, j, *_: (jnp.minimum(ratio * (i + 1), last_halo), col + j)),
        ]

    smem = pl.BlockSpec(memory_space=pltpu.SMEM)
    grid_spec = pltpu.PrefetchScalarGridSpec(
        num_scalar_prefetch=1,
        grid=(nblk, SWA_KV_HEADS),
        in_specs=[pl.BlockSpec((SWA_TQ, HEAD_DIM), qmap(g)) for g in range(SWA_GROUP)]
        + halo(COL_KC) + halo(COL_VC)
        + [pl.BlockSpec((SWA_TQ, w), lambda i, j, *_: (0, 0)), smem, smem,
           pl.BlockSpec(memory_space=pl.ANY)],
        out_specs=pl.BlockSpec((SWA_TQ, SWA_GROUP * HEAD_DIM),
                               lambda i, j, *_: (i, MIX_COL_C // SWA_GROUP + j)),
        scratch_shapes=[pltpu.VMEM((SWA_Q_HEADS, SWA_TQ, w), F32)],
    )
    assert MIX_COL_C % SWA_GROUP == 0
    return pl.pallas_call(
        _swa_kernel,
        grid_spec=grid_spec,
        out_shape=jax.ShapeDtypeStruct(mix.shape, mix.dtype),
        input_output_aliases={13: 0},
        compiler_params=_cparams(("arbitrary", "arbitrary")),
        name="sliding_window_attention",
    )(jnp.asarray(edge), *([h] * 9), bkt, rel_bias.astype(F32), sink.astype(F32), mix)


def encoder_trunk(x0, x1, p, seq_groups, w_in, w_out, na_rpb, lam_q1, lam_k1, lam_q2, lam_k2,
                  diff_g, sink, rel_bias, ffn1_gate, ffn1_up, ffn1_down,
                  ffn2_gate, ffn2_up, ffn2_down, ple_gate, ple_proj, ln_g, ln_b):
    x, xb = merge_streams(x0, x1)
    m = x.shape[0]
    seq_lens = [t for _, nb, t in seq_groups for _ in range(nb)]
    pb = p.astype(BF16)
    w_in, ple_gate, ple_proj = (cast_weights(w) for w in (w_in, ple_gate, ple_proj))
    w_out = cast_weights(w_out, row_rotate=A_W + B_W)
    ffn1_gate, ffn1_up, ffn2_gate, ffn2_up = (
        cast_weights(w, cols_out=D_FF_PAD) for w in (ffn1_gate, ffn1_up, ffn2_gate, ffn2_up))
    ffn1_down, ffn2_down = (cast_weights(w, rows_out=D_FF_PAD) for w in (ffn1_down, ffn2_down))
    ffn_tk = D_FF_PAD // 4
    for li in range(DEPTH):
        hid = gateup(xb, ffn1_gate, ffn1_up, li)
        y = matmul(hid, ffn1_down, li, F32, tk=ffn_tk, res=x, res_scale=DEEPNORM_ALPHA, acc_scale=0.5)
        x, xb = layernorm(y, ln_g[li, 0], ln_b[li, 0])
        h = matmul(xb, w_in, li, BF16)
        lambda_init = 0.8 - 0.6 * math.exp(-0.3 * li)
        lamv = jnp.stack([lam_q1[li], lam_k1[li], lam_q2[li], lam_k2[li]]).astype(F32)
        mix = jnp.zeros((m, MIX_W), BF16)
        mix = neighbourhood_attention(h, mix, na_rpb[li], seq_lens)
        for row0, nb, t in seq_groups:
            mix = differential_attention(h, mix, row0, nb, t, rel_bias, lamv, diff_g[li], lambda_init)
        mix = sliding_window_attention(h, mix, rel_bias, sink[li], seq_lens)
        y = matmul(mix, w_out, li, F32, res=x, res_scale=DEEPNORM_ALPHA)
        x, xb = layernorm(y, ln_g[li, 1], ln_b[li, 1])
        res = ple_residual(x, xb, pb, ple_gate, ple_proj, li)
        hid = gateup(xb, ffn2_gate, ffn2_up, li)
        y = matmul(hid, ffn2_down, li, F32, tk=ffn_tk, res=res, res_scale=1.0, acc_scale=0.5)
        if li < DEPTH - 1:
            x, xb = layernorm(y, ln_g[li, 2], ln_b[li, 2])
    return layernorm_split(y, ln_g[DEPTH - 1, 2], ln_b[DEPTH - 1, 2], x0.shape[0])


def kernel(x_prompt, x_sample, p_prompt, p_sample, w_in, w_out, na_rpb, lam_q1, lam_k1, lam_q2, lam_k2,
           diff_g, sink, rel_bias, ffn1_gate, ffn1_up, ffn1_down, ffn2_gate, ffn2_up, ffn2_down,
           ple_gate, ple_proj, ln_g, ln_b):
    bp, tp, d = x_prompt.shape
    bs, ts, _ = x_sample.shape
    mp, ms = bp * tp, bs * ts
    p = jnp.concatenate([p_prompt.reshape(DEPTH, mp, PLE_DIM), p_sample.reshape(DEPTH, ms, PLE_DIM)], axis=1)
    seq_groups = [(0, bp, tp), (mp, bs, ts)]
    y0, y1 = encoder_trunk(x_prompt.reshape(mp, d), x_sample.reshape(ms, d), p, seq_groups,
                           w_in, w_out, na_rpb, lam_q1, lam_k1, lam_q2, lam_k2,
                           diff_g, sink, rel_bias, ffn1_gate, ffn1_up, ffn1_down,
                           ffn2_gate, ffn2_up, ffn2_down, ple_gate, ple_proj, ln_g, ln_b)
    return y0.reshape(bp, tp, d), y1.reshape(bs, ts, d)
```

```python
import functools
import math

import jax
import jax.numpy as jnp
import numpy as np
from jax import lax
from jax.experimental import pallas as pl
from jax.experimental.pallas import tpu as pltpu

F32 = jnp.float32
BF16 = jnp.bfloat16

D_MODEL = 4096
DEPTH = 2
GRID_W = 64
HEAD_DIM = 128
NA_HEADS = 8
NA_WIN_R = 8
NA_WIN_C = 16
DIFF_HEADS = 12
DIFF_QK_DIM = 64
SWA_Q_HEADS = 12
SWA_KV_HEADS = 4
SWA_GROUP = SWA_Q_HEADS // SWA_KV_HEADS
SWA_WINDOW = 128
REL_BUCKETS = 32
REL_MAX_DIST = 128
D_FF = 11008
PLE_DIM = 256
LN_EPS = 1e-5
RMS_EPS = 1e-5
NEG_INF = -1e30
DEEPNORM_ALPHA = (2 * DEPTH) ** 0.25

A_W = NA_HEADS * HEAD_DIM
B_W = DIFF_HEADS * HEAD_DIM
C_Q_W = SWA_Q_HEADS * HEAD_DIM
C_KV_W = SWA_KV_HEADS * HEAD_DIM
MIX_W = A_W + B_W + C_Q_W
IN_W = 3 * A_W + 3 * B_W + C_Q_W + 2 * C_KV_W
COL_QA, COL_KA, COL_VA = 0, A_W // 128, 2 * A_W // 128
COL_QB = 3 * A_W // 128
COL_KB = COL_QB + B_W // 128
COL_VB = COL_KB + B_W // 128
COL_QC = COL_VB + B_W // 128
COL_KC = COL_QC + C_Q_W // 128
COL_VC = COL_KC + C_KV_W // 128
MIX_COL_C, MIX_COL_A, MIX_COL_B = 0, C_Q_W // 128, (C_Q_W + A_W) // 128

V7X_VMEM_BYTES = 64 * 1024 * 1024
VMEM_LIMIT = 56 * 1024 * 1024

FF_TILE = 512
D_FF_PAD = -(-D_FF // 1024) * 1024

NT_DIMS = (((1,), (1,)), ((), ()))


def _cparams(sem):
    return pltpu.CompilerParams(dimension_semantics=sem, vmem_limit_bytes=VMEM_LIMIT)


def _cast_kernel(w_ref, o_ref, *, rows, cols, masked):
    x = w_ref[0]
    if masked:
        tr, tc = x.shape
        r = pl.program_id(1) * tr + lax.broadcasted_iota(jnp.int32, x.shape, 0)
        c = pl.program_id(2) * tc + lax.broadcasted_iota(jnp.int32, x.shape, 1)
        x = jnp.where(jnp.logical_and(r < rows, c < cols), x, 0.0)
    o_ref[0] = x.astype(o_ref.dtype)


def cast_weights(w, rows_out=None, cols_out=None, row_rotate=0):
    nl, rows, cols = w.shape
    rows_out = rows if rows_out is None else rows_out
    cols_out = cols if cols_out is None else cols_out
    tr = math.gcd(math.gcd(rows_out, 1024), row_rotate)
    tc = math.gcd(cols_out, 2048)
    nrb = rows_out // tr
    assert row_rotate == 0 or rows_out == rows
    shift = row_rotate // tr
    spec = pl.BlockSpec((1, tr, tc), lambda l, i, j: (l, i, j))
    return pl.pallas_call(
        functools.partial(_cast_kernel, rows=rows, cols=cols,
                          masked=(rows_out != rows or cols_out != cols)),
        grid=(nl, nrb, cols_out // tc),
        in_specs=[pl.BlockSpec((1, tr, tc), lambda l, i, j: (l, (i + shift) % nrb, j))],
        out_specs=spec,
        out_shape=jax.ShapeDtypeStruct((nl, rows_out, cols_out), BF16),
        compiler_params=_cparams(("parallel", "parallel", "parallel")),
        name="cast_weights",
    )(w)


def _gateup_kernel(x_ref, wg_ref, wu_ref, o_ref):
    x = x_ref[...]
    g = jnp.dot(x, wg_ref[...], preferred_element_type=F32)
    u = jnp.dot(x, wu_ref[...], preferred_element_type=F32)
    o_ref[...] = (g * jax.nn.sigmoid(g) * u).astype(o_ref.dtype)


def gateup(x, wg, wu, li):
    m, k = x.shape
    f = wg.shape[2]
    tm = min(1024, m)
    tn = min(FF_TILE, f)
    return pl.pallas_call(
        _gateup_kernel,
        grid=(m // tm, f // tn),
        in_specs=[
            pl.BlockSpec((tm, k), lambda i, j: (i, 0)),
            pl.BlockSpec((None, k, tn), lambda i, j: (li, 0, j)),
            pl.BlockSpec((None, k, tn), lambda i, j: (li, 0, j)),
        ],
        out_specs=pl.BlockSpec((tm, tn), lambda i, j: (i, j)),
        out_shape=jax.ShapeDtypeStruct((m, f), BF16),
        compiler_params=_cparams(("parallel", "arbitrary")),
        name="gateup",
    )(x, wg, wu)


def _matmul_kernel(*refs, nk, res_scale, acc_scale):
    if res_scale is None:
        (a_ref, w_ref, o_ref, acc_ref), res_ref = refs, None
    else:
        a_ref, w_ref, res_ref, o_ref, acc_ref = refs
    kk = pl.program_id(2)

    def finish(total):
        if res_ref is not None:
            total = res_scale * res_ref[...] + acc_scale * total
        o_ref[...] = total.astype(o_ref.dtype)

    if nk == 1:
        finish(jnp.dot(a_ref[...], w_ref[...], preferred_element_type=F32))
        return

    @pl.when(kk == 0)
    def _():
        acc_ref[...] = jnp.zeros_like(acc_ref)

    acc_ref[...] += jnp.dot(a_ref[...], w_ref[...], preferred_element_type=F32)

    @pl.when(kk == nk - 1)
    def _():
        finish(acc_ref[...])


def matmul(a, w, li, out_dtype, tk=None, res=None, res_scale=None, acc_scale=1.0):
    m, k = a.shape
    n = w.shape[2]
    tm = min(1024, m)
    tn = min(1024, n)
    tk = k if tk is None else tk
    nk = k // tk
    in_specs = [
        pl.BlockSpec((tm, tk), lambda i, j, kk: (i, kk)),
        pl.BlockSpec((None, tk, tn), lambda i, j, kk: (li, kk, j)),
    ]
    operands = [a, w]
    if res is not None:
        in_specs.append(pl.BlockSpec((tm, tn), lambda i, j, kk: (i, j)))
        operands.append(res)
    return pl.pallas_call(
        functools.partial(_matmul_kernel, nk=nk, res_scale=res_scale if res is not None else None,
                          acc_scale=acc_scale),
        grid=(m // tm, n // tn, nk),
        in_specs=in_specs,
        out_specs=pl.BlockSpec((tm, tn), lambda i, j, kk: (i, j)),
        out_shape=jax.ShapeDtypeStruct((m, n), out_dtype),
        scratch_shapes=[pltpu.VMEM((tm, tn) if nk > 1 else (8, 128), F32)],
        compiler_params=_cparams(("parallel", "parallel", "arbitrary")),
        name="matmul",
    )(*operands)


def _ple_kernel(x_ref, xb_ref, p_ref, wg_ref, wp_ref, o_ref):
    gate = jax.nn.sigmoid(jnp.dot(xb_ref[...], wg_ref[...], preferred_element_type=F32))
    proj = jnp.dot(p_ref[...], wp_ref[...], preferred_element_type=F32)
    o_ref[...] = DEEPNORM_ALPHA * x_ref[...] + gate * proj


def ple_residual(x, xb, p, wg, wp, li):
    m, d = x.shape
    pd = p.shape[2]
    tm = min(512, m)
    tn = min(1024, d)
    return pl.pallas_call(
        _ple_kernel,
        grid=(m // tm, d // tn),
        in_specs=[
            pl.BlockSpec((tm, tn), lambda i, j: (i, j)),
            pl.BlockSpec((tm, d), lambda i, j: (i, 0)),
            pl.BlockSpec((None, tm, pd), lambda i, j: (li, i, 0)),
            pl.BlockSpec((None, d, tn), lambda i, j: (li, 0, j)),
            pl.BlockSpec((None, pd, tn), lambda i, j: (li, 0, j)),
        ],
        out_specs=pl.BlockSpec((tm, tn), lambda i, j: (i, j)),
        out_shape=jax.ShapeDtypeStruct((m, d), F32),
        compiler_params=_cparams(("parallel", "arbitrary")),
        name="ple_residual",
    )(x, xb, p, wg, wp)


def _layer_norm(y, g, b):
    mu = jnp.mean(y, axis=-1, keepdims=True)
    yc = y - mu
    var = jnp.mean(yc * yc, axis=-1, keepdims=True)
    return yc * lax.rsqrt(var + LN_EPS) * g + b


def _ln_kernel(y_ref, g_ref, b_ref, of_ref, ob_ref):
    out = _layer_norm(y_ref[...], g_ref[...], b_ref[...])
    of_ref[...] = out
    ob_ref[...] = out.astype(BF16)


LN_ROWS = 256


def layernorm(y, g, b):
    m, d = y.shape
    tm = min(LN_ROWS, m)
    row = pl.BlockSpec((tm, d), lambda i: (i, 0))
    vec = pl.BlockSpec((1, d), lambda i: (0, 0))
    return pl.pallas_call(
        _ln_kernel,
        grid=(m // tm,),
        in_specs=[row, vec, vec],
        out_specs=[row, row],
        out_shape=[jax.ShapeDtypeStruct((m, d), F32), jax.ShapeDtypeStruct((m, d), BF16)],
        compiler_params=_cparams(("parallel",)),
        name="layernorm",
    )(y, g.reshape(1, d), b.reshape(1, d))


def _ln_split_kernel(y_ref, g_ref, b_ref, o0_ref, o1_ref, *, n0):
    i = pl.program_id(0)
    out = _layer_norm(y_ref[...], g_ref[...], b_ref[...])

    @pl.when(i < n0)
    def _():
        o0_ref[...] = out

    @pl.when(i >= n0)
    def _():
        o1_ref[...] = out


def layernorm_split(y, g, b, m0):
    m, d = y.shape
    tm = min(LN_ROWS, m0, m - m0)
    assert m0 % tm == 0 and (m - m0) % tm == 0
    n0 = m0 // tm
    vec = pl.BlockSpec((1, d), lambda i: (0, 0))
    return pl.pallas_call(
        functools.partial(_ln_split_kernel, n0=n0),
        grid=(m // tm,),
        in_specs=[pl.BlockSpec((tm, d), lambda i: (i, 0)), vec, vec],
        out_specs=[pl.BlockSpec((tm, d), lambda i: (jnp.minimum(i, n0 - 1), 0)),
                   pl.BlockSpec((tm, d), lambda i: (jnp.maximum(i - n0, 0), 0))],
        out_shape=[jax.ShapeDtypeStruct((m0, d), F32), jax.ShapeDtypeStruct((m - m0, d), F32)],
        compiler_params=_cparams(("arbitrary",)),
        name="layernorm_split",
    )(y, g.reshape(1, d), b.reshape(1, d))


def _merge_kernel(x0_ref, x1_ref, of_ref, ob_ref, *, n0):
    x = jnp.where(pl.program_id(0) < n0, x0_ref[...], x1_ref[...])
    of_ref[...] = x
    ob_ref[...] = x.astype(BF16)


def merge_streams(x0, x1):
    m0, d = x0.shape
    m1 = x1.shape[0]
    tm = min(LN_ROWS, m0, m1)
    assert m0 % tm == 0 and m1 % tm == 0
    n0 = m0 // tm
    row = pl.BlockSpec((tm, d), lambda i: (i, 0))
    return pl.pallas_call(
        functools.partial(_merge_kernel, n0=n0),
        grid=((m0 + m1) // tm,),
        in_specs=[pl.BlockSpec((tm, d), lambda i: (jnp.minimum(i, n0 - 1), 0)),
                  pl.BlockSpec((tm, d), lambda i: (jnp.maximum(i - n0, 0), 0))],
        out_specs=[row, row],
        out_shape=[jax.ShapeDtypeStruct((m0 + m1, d), F32), jax.ShapeDtypeStruct((m0 + m1, d), BF16)],
        compiler_params=_cparams(("parallel",)),
        name="merge_streams",
    )(x0, x1)


def _t5_bucket(rel):
    half = REL_BUCKETS // 2
    max_exact = half // 2
    n = jnp.abs(rel)
    side = jnp.where(rel > 0, half, 0)
    nf = jnp.maximum(n, 1).astype(F32)
    large = max_exact + (jnp.log(nf / max_exact) / math.log(REL_MAX_DIST / max_exact)
                         * (half - max_exact)).astype(jnp.int32)
    large = jnp.minimum(large, half - 1)
    return side + jnp.where(n < max_exact, n, large)


DIFF_TILE = 256
DIFF_ONES_ROWS = 16
DIFF_GROUP = 4
SWA_TQ = 512
LOG2E = math.log2(math.e)


def _bucket_select(bucket, table_ref, col, init):
    out = jnp.full(bucket.shape, init, F32)
    for b in range(REL_BUCKETS):
        out = jnp.where(bucket == b, table_ref[b, col], out)
    return out


NA_QROWS = 8
NA_QBLK = NA_QROWS * GRID_W
NA_KBLK = NA_QBLK // 2
NA_RPB_C = 2 * NA_WIN_C - 1


NA_SPAN_ROWS = 2 * NA_QROWS


def _na_build_bias(rpb_ref, tab_ref, hd):
    shape = (GRID_W, 2 * GRID_W)
    lane = lax.broadcasted_iota(jnp.int32, shape, 1)
    qc = lax.broadcasted_iota(jnp.int32, shape, 0)
    second = lane >= GRID_W
    kc = jnp.where(second, lane - GRID_W, lane)
    d = kc - qc + (NA_WIN_C - 1)
    c0 = jnp.clip(qc - NA_WIN_C // 2, 0, GRID_W - NA_WIN_C)
    col_ok = jnp.logical_and(kc >= c0, kc < c0 + NA_WIN_C)
    pairs = []
    for r in range(2 * NA_WIN_R - 2):
        t2 = jnp.full(shape, NEG_INF, F32)
        for dd in range(NA_RPB_C):
            val = jnp.where(second, rpb_ref[hd, (r + 1) * NA_RPB_C + dd], rpb_ref[hd, r * NA_RPB_C + dd])
            t2 = jnp.where(d == dd, val, t2)
        pairs.append(jnp.where(col_ok, t2, NEG_INF))
    masked = jnp.full(shape, NEG_INF, F32)
    half = NA_WIN_R // 2
    for variant in range(3):
        for il in range(NA_QROWS):
            back = (min(il, half), half, max(il, half))[variant]
            start = il + half - back
            a0 = NA_WIN_R - 1 - back
            for jl in range(0, NA_SPAN_ROWS, 2):
                ok0 = start <= jl < start + NA_WIN_R
                ok1 = start <= jl + 1 < start + NA_WIN_R
                r = a0 + jl - start
                assert not (ok0 or ok1) or 0 <= r < len(pairs)
                if ok0 and ok1:
                    tile = pairs[r]
                elif ok0:
                    tile = jnp.where(second, NEG_INF, pairs[r])
                elif ok1:
                    tile = jnp.where(second, pairs[r], NEG_INF)
                else:
                    tile = masked
                tab_ref[variant, il * GRID_W:(il + 1) * GRID_W, jl * GRID_W:(jl + 2) * GRID_W] = tile


def _na_kernel(kidx_ref, var_ref, q_ref, k0, k1, k2, k3, v0, v1, v2, v3, rpb_ref, mix_ref, o_ref, tab_ref):
    del mix_ref
    hd = pl.program_id(0)
    n = pl.program_id(1)
    scale = HEAD_DIM ** -0.5

    pl.when(n == 0)(functools.partial(_na_build_bias, rpb_ref, tab_ref, hd))

    kspan = jnp.concatenate([k0[...], k1[...], k2[...], k3[...]], axis=0)
    vspan = jnp.concatenate([v0[...], v1[...], v2[...], v3[...]], axis=0)
    s = lax.dot_general(q_ref[...], kspan, NT_DIMS, preferred_element_type=F32) * scale + tab_ref[var_ref[n]]
    m = jnp.max(s, axis=-1, keepdims=True)
    p = jnp.exp(s - m)
    l = jnp.sum(p, axis=-1, keepdims=True)
    o = jnp.dot(p.astype(BF16), vspan, preferred_element_type=F32) / l
    o_ref[...] = o.astype(o_ref.dtype)


def neighbourhood_attention(h, mix, rpb, seq_lens):
    m = h.shape[0]
    nblk = m // NA_QBLK
    kidx = np.zeros((4, nblk), np.int32)
    variant = np.ones((nblk,), np.int32)
    base = 0
    for t in seq_lens:
        nb = t // NA_QBLK
        assert nb >= 2 and t % NA_QBLK == 0
        lo, hi = 2 * base, 2 * (base + nb) - 1
        for n in range(nb):
            for j in range(4):
                kidx[j, base + n] = np.clip(2 * (base + n) - 1 + j, lo, hi)
        variant[base] = 0
        variant[base + nb - 1] = 2
        base += nb

    def kmap(j, col):
        return lambda hd, n, kidx_ref, var_ref: (kidx_ref[j, n], col + hd)

    kv_specs = [pl.BlockSpec((NA_KBLK, HEAD_DIM), kmap(j, COL_KA)) for j in range(4)]
    kv_specs += [pl.BlockSpec((NA_KBLK, HEAD_DIM), kmap(j, COL_VA)) for j in range(4)]
    grid_spec = pltpu.PrefetchScalarGridSpec(
        num_scalar_prefetch=2,
        grid=(NA_HEADS, nblk),
        in_specs=[pl.BlockSpec((NA_QBLK, HEAD_DIM), lambda hd, n, *_: (n, COL_QA + hd))]
        + kv_specs
        + [pl.BlockSpec(memory_space=pltpu.SMEM), pl.BlockSpec(memory_space=pl.ANY)],
        out_specs=pl.BlockSpec((NA_QBLK, HEAD_DIM), lambda hd, n, *_: (n, MIX_COL_A + hd)),
        scratch_shapes=[pltpu.VMEM((3, NA_QBLK, NA_SPAN_ROWS * GRID_W), F32)],
    )
    return pl.pallas_call(
        _na_kernel,
        grid_spec=grid_spec,
        out_shape=jax.ShapeDtypeStruct(mix.shape, mix.dtype),
        input_output_aliases={12: 0},
        compiler_params=_cparams(("arbitrary", "arbitrary")),
        name="neighbourhood_attention",
    )(jnp.asarray(kidx), jnp.asarray(variant), h, *([h] * 8),
      rpb.reshape(NA_HEADS, -1).astype(F32), mix)


def _diff_kernel(far_ref, rel_ref, q_ref, qn_ref, k_ref, v_ref, bkt_ref, lamv_ref, g_ref, mix_ref, o_ref,
                 s0_ref, s1_ref, p0_ref, p1_ref, vt_ref, bias_ref, m0_ref, *, t, nk, lambda_init):
    del mix_ref
    hd = pl.program_id(1)
    i = pl.program_id(2)
    kg = DIFF_GROUP * t
    ngroups = nk // DIFF_GROUP
    neg = jnp.full((1, t), NEG_INF, F32)

    def component(q, c):
        lane = lax.broadcasted_iota(jnp.int32, q.shape, 1)
        keep = lane >= DIFF_QK_DIM if c else lane < DIFF_QK_DIM
        return jnp.where(keep, q, jnp.zeros_like(q)) * jnp.asarray(DIFF_QK_DIM ** -0.5, BF16)

    c_lo = rel_ref[far_ref[0], hd] * LOG2E
    c_hi = rel_ref[far_ref[1], hd] * LOG2E

    def far_bias(j, near, tile):
        return jnp.where(j < tile - 1, c_lo, jnp.where(j > tile + 1, c_hi, near))

    def scores(qc, s_ref, g, m, tile):
        lo = g * kg
        s = lax.dot_general(k_ref[lo:lo + kg, :], qc, NT_DIMS, preferred_element_type=F32) * LOG2E
        s_ref[lo:lo + kg, :] = s
        for u in range(DIFF_GROUP):
            m = jnp.maximum(m, jnp.max(s[u * t:(u + 1) * t], axis=0, keepdims=True)
                            + far_bias(g * DIFF_GROUP + u, NEG_INF, tile))
        return m

    def add_diagonal_bias(s_ref, m, tile):
        for o in (-1, 0, 1):
            j = tile + o
            ok = jnp.logical_and(j >= 0, j < nk)
            row = pl.multiple_of(jnp.clip(j, 0, nk - 1) * t, t)
            blk = s_ref[pl.ds(row, t), :] + jnp.where(ok, bias_ref[o + 1], 0.0)
            s_ref[pl.ds(row, t), :] = blk
            m = jnp.maximum(m, jnp.max(blk, axis=0, keepdims=True) + jnp.where(ok, 0.0, NEG_INF))
        return m

    def exp_group(s_ref, p_ref, m, g):
        for jc in range(g * DIFF_GROUP, (g + 1) * DIFF_GROUP):
            p = jnp.exp2(s_ref[jc * t:(jc + 1) * t, :] - (m - far_bias(jc, 0.0, i)))
            p_ref[jc * t:(jc + 1) * t, :] = p.astype(BF16)

    @pl.when(i == 0)
    def _prepare_head():
        for o in range(3):
            bias_ref[o] = _bucket_select(bkt_ref[o], rel_ref, hd, 0.0) * LOG2E
        for c in range(nk):
            vt_ref[:HEAD_DIM, c * t:(c + 1) * t] = v_ref[c * t:(c + 1) * t, :].astype(F32).T.astype(BF16)
        vt_ref[HEAD_DIM:, :] = jnp.ones((DIFF_ONES_ROWS, nk * t), BF16)
        q0 = component(q_ref[...], 0)
        m = neg
        for g in range(ngroups):
            m = scores(q0, s0_ref, g, m, 0)
        m0_ref[...] = add_diagonal_bias(s0_ref, m, 0)

    m0 = m0_ref[...]
    q1 = component(q_ref[...], 1)
    m1 = neg
    for g in range(ngroups):
        m1 = scores(q1, s1_ref, g, m1, i)
        exp_group(s0_ref, p0_ref, m0, g)
    m1 = add_diagonal_bias(s1_ref, m1, i)
    qn0 = component(qn_ref[...], 0)
    pv0 = jnp.zeros((HEAD_DIM + DIFF_ONES_ROWS, t), F32)
    m0n = neg
    for g in range(ngroups):
        pv0 = pv0 + jnp.dot(vt_ref[:, g * kg:(g + 1) * kg], p0_ref[g * kg:(g + 1) * kg, :],
                            preferred_element_type=F32)
        exp_group(s1_ref, p1_ref, m1, g)
        m0n = scores(qn0, s0_ref, g, m0n, i + 1)
    m0_ref[...] = add_diagonal_bias(s0_ref, m0n, i + 1)
    half = (ngroups + 1) // 2 * kg
    pv1 = jnp.dot(vt_ref[:, :half], p1_ref[:half, :], preferred_element_type=F32)
    if half < nk * t:
        pv1 = pv1 + jnp.dot(vt_ref[:, half:], p1_ref[half:, :], preferred_element_type=F32)
    pv = [pv0, pv1]
    acc = [x[:HEAD_DIM] for x in pv]
    l = [x[HEAD_DIM:HEAD_DIM + 1] for x in pv]
    lamv = lamv_ref[...]
    lam = (jnp.exp(jnp.sum(lamv[0:1] * lamv[1:2], axis=-1, keepdims=True))
           - jnp.exp(jnp.sum(lamv[2:3] * lamv[3:4], axis=-1, keepdims=True)) + lambda_init)
    o_t = acc[0] / l[0] - lam * (acc[1] / l[1])
    ms = jnp.mean(o_t * o_t, axis=0, keepdims=True)
    o_t = o_t * lax.rsqrt(ms + RMS_EPS) * g_ref[...] * (1.0 - lambda_init)
    o_ref[...] = o_t.T.astype(o_ref.dtype)


def differential_attention(h, mix, row0, nb, t_seq, rel_bias, lamv, sub_g, lambda_init):
    t = DIFF_TILE
    nq = t_seq // t
    assert row0 % t_seq == 0 and t_seq % t == 0
    qb0, sb0 = row0 // t, row0 // t_seq
    kq = np.arange(t)[:, None] - np.arange(t)[None, :]
    bkt = _t5_bucket(jnp.asarray(np.stack([kq + o * t for o in (-1, 0, 1)]), dtype=jnp.int32))
    far = _t5_bucket(jnp.asarray([-2 * t, 2 * t], dtype=jnp.int32))
    smem = pl.BlockSpec(memory_space=pltpu.SMEM)
    return pl.pallas_call(
        functools.partial(_diff_kernel, t=t, nk=nq, lambda_init=lambda_init),
        grid=(nb, DIFF_HEADS, nq),
        in_specs=[
            smem, smem,
            pl.BlockSpec((t, HEAD_DIM), lambda b, hd, i: (qb0 + b * nq + i, COL_QB + hd)),
            pl.BlockSpec((t, HEAD_DIM),
                         lambda b, hd, i: (qb0 + b * nq + jnp.minimum(i + 1, nq - 1), COL_QB + hd)),
            pl.BlockSpec((t_seq, HEAD_DIM), lambda b, hd, i: (sb0 + b, COL_KB + hd)),
            pl.BlockSpec((t_seq, HEAD_DIM), lambda b, hd, i: (sb0 + b, COL_VB + hd)),
            pl.BlockSpec((3, t, t), lambda b, hd, i: (0, 0, 0)),
            pl.BlockSpec((4, DIFF_QK_DIM), lambda b, hd, i: (0, 0)),
            pl.BlockSpec((HEAD_DIM, 1), lambda b, hd, i: (0, 0)),
            pl.BlockSpec(memory_space=pl.ANY),
        ],
        out_specs=pl.BlockSpec((t, HEAD_DIM), lambda b, hd, i: (qb0 + b * nq + i, MIX_COL_B + hd)),
        scratch_shapes=[
            pltpu.VMEM((t_seq, t), F32), pltpu.VMEM((t_seq, t), F32),
            pltpu.VMEM((t_seq, t), BF16), pltpu.VMEM((t_seq, t), BF16),
            pltpu.VMEM((HEAD_DIM + DIFF_ONES_ROWS, t_seq), BF16),
            pltpu.VMEM((3, t, t), F32),
            pltpu.VMEM((1, t), F32),
        ],
        out_shape=jax.ShapeDtypeStruct(mix.shape, mix.dtype),
        input_output_aliases={9: 0},
        compiler_params=_cparams(("arbitrary", "arbitrary", "arbitrary")),
        name="differential_attention",
    )(far, rel_bias.astype(F32), h, h, h, h, bkt, lamv, sub_g.reshape(HEAD_DIM, 1).astype(F32), mix)


def _swa_kernel(edge_ref, q0, q1, q2, kp, kc, kn, vp, vc, vn, bkt_ref, rel_ref, sink_ref, mix_ref, o_ref,
                bias_ref):
    del mix_ref
    i = pl.program_id(0)
    j = pl.program_id(1)
    scale = HEAD_DIM ** -0.5
    w = SWA_TQ + 2 * SWA_WINDOW

    @pl.when(i == 0)
    def _build_bias():
        for g in range(SWA_GROUP):
            head = j * SWA_GROUP + g
            bias_ref[head] = _bucket_select(bkt_ref[...], rel_ref, DIFF_HEADS + head, NEG_INF)

    kwin = jnp.concatenate([kp[...], kc[...], kn[...]], axis=0)
    vwin = jnp.concatenate([vp[...], vc[...], vn[...]], axis=0)
    col = lax.broadcasted_iota(jnp.int32, (1, w), 1)
    is_first = edge_ref[0, i] == 1
    is_last = edge_ref[1, i] == 1
    valid = jnp.logical_and(jnp.logical_or(col >= SWA_WINDOW, jnp.logical_not(is_first)),
                            jnp.logical_or(col < SWA_TQ + SWA_WINDOW, jnp.logical_not(is_last)))
    for g, q_ref in enumerate((q0, q1, q2)):
        head = j * SWA_GROUP + g
        sink = sink_ref[head]
        s = lax.dot_general(q_ref[...], kwin, NT_DIMS, preferred_element_type=F32) * scale + bias_ref[head]
        s = jnp.where(valid, s, NEG_INF)
        m = jnp.maximum(jnp.max(s, axis=-1, keepdims=True), sink)
        p = jnp.exp(s - m)
        l = jnp.sum(p, axis=-1, keepdims=True) + jnp.exp(sink - m)
        o = jnp.dot(p.astype(BF16), vwin, preferred_element_type=F32) / l
        o_ref[:, g * HEAD_DIM:(g + 1) * HEAD_DIM] = o.astype(o_ref.dtype)


def sliding_window_attention(h, mix, rel_bias, sink, seq_lens):
    m = h.shape[0]
    nblk = m // SWA_TQ
    ratio = SWA_TQ // SWA_WINDOW
    edge = np.zeros((2, nblk), np.int32)
    base = 0
    for t in seq_lens:
        assert t % SWA_TQ == 0
        edge[0, base] = 1
        base += t // SWA_TQ
        edge[1, base - 1] = 1
    last_halo = m // SWA_WINDOW - 1
    w = SWA_TQ + 2 * SWA_WINDOW
    rel = (np.arange(w)[None, :] - SWA_WINDOW) - np.arange(SWA_TQ)[:, None]
    bkt = jnp.where(jnp.asarray(np.abs(rel) <= SWA_WINDOW), _t5_bucket(jnp.asarray(rel, dtype=jnp.int32)), -1)

    def qmap(g):
        return lambda i, j, *_: (i, COL_QC + j * SWA_GROUP + g)

    def halo(col):
        return [
            pl.BlockSpec((SWA_WINDOW, HEAD_DIM), lambda i, j, *_: (jnp.maximum(ratio * i - 1, 0), col + j)),
            pl.BlockSpec((SWA_TQ, HEAD_DIM), lambda i, j, *_: (i, col + j)),
            pl.BlockSpec((SWA_WINDOW, HEAD_DIM),
                         lambda i, j, *_: (jnp.minimum(ratio * (i + 1), last_halo), col + j)),
        ]

    smem = pl.BlockSpec(memory_space=pltpu.SMEM)
    grid_spec = pltpu.PrefetchScalarGridSpec(
        num_scalar_prefetch=1,
        grid=(nblk, SWA_KV_HEADS),
        in_specs=[pl.BlockSpec((SWA_TQ, HEAD_DIM), qmap(g)) for g in range(SWA_GROUP)]
        + halo(COL_KC) + halo(COL_VC)
        + [pl.BlockSpec((SWA_TQ, w), lambda i, j, *_: (0, 0)), smem, smem,
           pl.BlockSpec(memory_space=pl.ANY)],
        out_specs=pl.BlockSpec((SWA_TQ, SWA_GROUP * HEAD_DIM),
                               lambda i, j, *_: (i, MIX_COL_C // SWA_GROUP + j)),
        scratch_shapes=[pltpu.VMEM((SWA_Q_HEADS, SWA_TQ, w), F32)],
    )
    assert MIX_COL_C % SWA_GROUP == 0
    return pl.pallas_call(
        _swa_kernel,
        grid_spec=grid_spec,
        out_shape=jax.ShapeDtypeStruct(mix.shape, mix.dtype),
        input_output_aliases={13: 0},
        compiler_params=_cparams(("arbitrary", "arbitrary")),
        name="sliding_window_attention",
    )(jnp.asarray(edge), *([h] * 9), bkt, rel_bias.astype(F32), sink.astype(F32), mix)


def encoder_trunk(x0, x1, p, seq_groups, w_in, w_out, na_rpb, lam_q1, lam_k1, lam_q2, lam_k2,
                  diff_g, sink, rel_bias, ffn1_gate, ffn1_up, ffn1_down,
                  ffn2_gate, ffn2_up, ffn2_down, ple_gate, ple_proj, ln_g, ln_b):
    x, xb = merge_streams(x0, x1)
    m = x.shape[0]
    seq_lens = [t for _, nb, t in seq_groups for _ in range(nb)]
    pb = p.astype(BF16)
    w_in, ple_gate, ple_proj = (cast_weights(w) for w in (w_in, ple_gate, ple_proj))
    w_out = cast_weights(w_out, row_rotate=A_W + B_W)
    ffn1_gate, ffn1_up, ffn2_gate, ffn2_up = (
        cast_weights(w, cols_out=D_FF_PAD) for w in (ffn1_gate, ffn1_up, ffn2_gate, ffn2_up))
    ffn1_down, ffn2_down = (cast_weights(w, rows_out=D_FF_PAD) for w in (ffn1_down, ffn2_down))
    ffn_tk = D_FF_PAD // 4
    for li in range(DEPTH):
        hid = gateup(xb, ffn1_gate, ffn1_up, li)
        y = matmul(hid, ffn1_down, li, F32, tk=ffn_tk, res=x, res_scale=DEEPNORM_ALPHA, acc_scale=0.5)
        x, xb = layernorm(y, ln_g[li, 0], ln_b[li, 0])
        h = matmul(xb, w_in, li, BF16)
        lambda_init = 0.8 - 0.6 * math.exp(-0.3 * li)
        lamv = jnp.stack([lam_q1[li], lam_k1[li], lam_q2[li], lam_k2[li]]).astype(F32)
        mix = jnp.zeros((m, MIX_W), BF16)
        mix = neighbourhood_attention(h, mix, na_rpb[li], seq_lens)
        for row0, nb, t in seq_groups:
            mix = differential_attention(h, mix, row0, nb, t, rel_bias, lamv, diff_g[li], lambda_init)
        mix = sliding_window_attention(h, mix, rel_bias, sink[li], seq_lens)
        y = matmul(mix, w_out, li, F32, res=x, res_scale=DEEPNORM_ALPHA)
        x, xb = layernorm(y, ln_g[li, 1], ln_b[li, 1])
        res = ple_residual(x, xb, pb, ple_gate, ple_proj, li)
        hid = gateup(xb, ffn2_gate, ffn2_up, li)
        y = matmul(hid, ffn2_down, li, F32, tk=ffn_tk, res=res, res_scale=1.0, acc_scale=0.5)
        if li < DEPTH - 1:
            x, xb = layernorm(y, ln_g[li, 2], ln_b[li, 2])
    return layernorm_split(y, ln_g[DEPTH - 1, 2], ln_b[DEPTH - 1, 2], x0.shape[0])


def kernel(x_prompt, x_sample, p_prompt, p_sample, w_in, w_out, na_rpb, lam_q1, lam_k1, lam_q2, lam_k2,
           diff_g, sink, rel_bias, ffn1_gate, ffn1_up, ffn1_down, ffn2_gate, ffn2_up, ffn2_down,
           ple_gate, ple_proj, ln_g, ln_b):
    bp, tp, d = x_prompt.shape
    bs, ts, _ = x_sample.shape
    mp, ms = bp * tp, bs * ts
    p = jnp.concatenate([p_prompt.reshape(DEPTH, mp, PLE_DIM), p_sample.reshape(DEPTH, ms, PLE_DIM)], axis=1)
    seq_groups = [(0, bp, tp), (mp, bs, ts)]
    y0, y1 = encoder_trunk(x_prompt.reshape(mp, d), x_sample.reshape(ms, d), p, seq_groups,
                           w_in, w_out, na_rpb, lam_q1, lam_k1, lam_q2, lam_k2,
                           diff_g, sink, rel_bias, ffn1_gate, ffn1_up, ffn1_down,
                           ffn2_gate, ffn2_up, ffn2_down, ple_gate, ple_proj, ln_g, ln_b)
    return y0.reshape(bp, tp, d), y1.reshape(bs, ts, d)
```

```python
import functools
import math

import jax
import jax.numpy as jnp
import numpy as np
from jax import lax
from jax.experimental import pallas as pl
from jax.experimental.pallas import tpu as pltpu

F32 = jnp.float32
BF16 = jnp.bfloat16

DEPTH = 2
GRID_W = 64
HEAD_DIM = 128
NA_HEADS = 8
NA_WIN_R = 8
NA_WIN_C = 16
DIFF_HEADS = 12
DIFF_QK_DIM = 64
SWA_Q_HEADS = 12
SWA_KV_HEADS = 4
SWA_GROUP = SWA_Q_HEADS // SWA_KV_HEADS
SWA_WINDOW = 128
REL_BUCKETS = 32
REL_MAX_DIST = 128
D_FF = 11008
PLE_DIM = 256
LN_EPS = 1e-5
RMS_EPS = 1e-5
NEG_INF = -1e30
DEEPNORM_ALPHA = (2 * DEPTH) ** 0.25

A_W = NA_HEADS * HEAD_DIM
B_W = DIFF_HEADS * HEAD_DIM
C_Q_W = SWA_Q_HEADS * HEAD_DIM
C_KV_W = SWA_KV_HEADS * HEAD_DIM
MIX_W = A_W + B_W + C_Q_W
IN_W = 3 * A_W + 3 * B_W + C_Q_W + 2 * C_KV_W
COL_QA, COL_KA, COL_VA = 0, A_W // 128, 2 * A_W // 128
COL_QB = 3 * A_W // 128
COL_KB = COL_QB + B_W // 128
COL_VB = COL_KB + B_W // 128
COL_QC = COL_VB + B_W // 128
COL_KC = COL_QC + C_Q_W // 128
COL_VC = COL_KC + C_KV_W // 128
MIX_COL_C, MIX_COL_A, MIX_COL_B = 0, C_Q_W // 128, (C_Q_W + A_W) // 128

V7X_VMEM_BYTES = 64 * 1024 * 1024
VMEM_LIMIT = V7X_VMEM_BYTES - 8 * 1024 * 1024

FF_TILE = 512
D_FF_PAD = -(-D_FF // 1024) * 1024

NT_DIMS = (((1,), (1,)), ((), ()))


def _cparams(sem):
    return pltpu.CompilerParams(dimension_semantics=sem, vmem_limit_bytes=VMEM_LIMIT)


def _cast_kernel(w_ref, o_ref, *, rows, cols, masked):
    x = w_ref[0]
    if masked:
        tr, tc = x.shape
        r = pl.program_id(1) * tr + lax.broadcasted_iota(jnp.int32, x.shape, 0)
        c = pl.program_id(2) * tc + lax.broadcasted_iota(jnp.int32, x.shape, 1)
        x = jnp.where(jnp.logical_and(r < rows, c < cols), x, 0.0)
    o_ref[0] = x.astype(o_ref.dtype)


def cast_weights(w, rows_out=None, cols_out=None, row_rotate=0):
    nl, rows, cols = w.shape
    rows_out = rows if rows_out is None else rows_out
    cols_out = cols if cols_out is None else cols_out
    tr = math.gcd(math.gcd(rows_out, 1024), row_rotate)
    tc = math.gcd(cols_out, 2048)
    nrb = rows_out // tr
    assert row_rotate == 0 or rows_out == rows
    shift = row_rotate // tr
    spec = pl.BlockSpec((1, tr, tc), lambda l, i, j: (l, i, j))
    return pl.pallas_call(
        functools.partial(_cast_kernel, rows=rows, cols=cols,
                          masked=(rows_out != rows or cols_out != cols)),
        grid=(nl, nrb, cols_out // tc),
        in_specs=[pl.BlockSpec((1, tr, tc), lambda l, i, j: (l, (i + shift) % nrb, j))],
        out_specs=spec,
        out_shape=jax.ShapeDtypeStruct((nl, rows_out, cols_out), BF16),
        compiler_params=_cparams(("parallel", "parallel", "parallel")),
        name="cast_weights",
    )(w)


def _gateup_kernel(x_ref, wg_ref, wu_ref, o_ref):
    x = x_ref[...]
    g = jnp.dot(x, wg_ref[...], preferred_element_type=F32)
    u = jnp.dot(x, wu_ref[...], preferred_element_type=F32)
    o_ref[...] = (g * jax.nn.sigmoid(g) * u).astype(o_ref.dtype)


def gateup(x, wg, wu, li):
    m, k = x.shape
    f = wg.shape[2]
    tm = min(1024, m)
    tn = min(FF_TILE, f)
    return pl.pallas_call(
        _gateup_kernel,
        grid=(m // tm, f // tn),
        in_specs=[
            pl.BlockSpec((tm, k), lambda i, j: (i, 0)),
            pl.BlockSpec((None, k, tn), lambda i, j: (li, 0, j)),
            pl.BlockSpec((None, k, tn), lambda i, j: (li, 0, j)),
        ],
        out_specs=pl.BlockSpec((tm, tn), lambda i, j: (i, j)),
        out_shape=jax.ShapeDtypeStruct((m, f), BF16),
        compiler_params=_cparams(("parallel", "arbitrary")),
        name="gateup",
    )(x, wg, wu)


def _matmul_kernel(*refs, nk, res_scale, acc_scale):
    if res_scale is None:
        (a_ref, w_ref, o_ref, acc_ref), res_ref = refs, None
    else:
        a_ref, w_ref, res_ref, o_ref, acc_ref = refs
    kk = pl.program_id(2)

    def finish(total):
        if res_ref is not None:
            total = res_scale * res_ref[...] + acc_scale * total
        o_ref[...] = total.astype(o_ref.dtype)

    if nk == 1:
        finish(jnp.dot(a_ref[...], w_ref[...], preferred_element_type=F32))
        return

    @pl.when(kk == 0)
    def _():
        acc_ref[...] = jnp.zeros_like(acc_ref)

    acc_ref[...] += jnp.dot(a_ref[...], w_ref[...], preferred_element_type=F32)

    @pl.when(kk == nk - 1)
    def _():
        finish(acc_ref[...])


def matmul(a, w, li, out_dtype, tk=None, res=None, res_scale=None, acc_scale=1.0):
    m, k = a.shape
    n = w.shape[2]
    tm = min(1024, m)
    tn = min(1024, n)
    tk = k if tk is None else tk
    nk = k // tk
    in_specs = [
        pl.BlockSpec((tm, tk), lambda i, j, kk: (i, kk)),
        pl.BlockSpec((None, tk, tn), lambda i, j, kk: (li, kk, j)),
    ]
    operands = [a, w]
    if res is not None:
        in_specs.append(pl.BlockSpec((tm, tn), lambda i, j, kk: (i, j)))
        operands.append(res)
    return pl.pallas_call(
        functools.partial(_matmul_kernel, nk=nk, res_scale=res_scale if res is not None else None,
                          acc_scale=acc_scale),
        grid=(m // tm, n // tn, nk),
        in_specs=in_specs,
        out_specs=pl.BlockSpec((tm, tn), lambda i, j, kk: (i, j)),
        out_shape=jax.ShapeDtypeStruct((m, n), out_dtype),
        scratch_shapes=[pltpu.VMEM((tm, tn) if nk > 1 else (8, 128), F32)],
        compiler_params=_cparams(("parallel", "parallel", "arbitrary")),
        name="matmul",
    )(*operands)


def _ple_kernel(x_ref, xb_ref, p_ref, wg_ref, wp_ref, o_ref):
    gate = jax.nn.sigmoid(jnp.dot(xb_ref[...], wg_ref[...], preferred_element_type=F32))
    proj = jnp.dot(p_ref[...], wp_ref[...], preferred_element_type=F32)
    o_ref[...] = DEEPNORM_ALPHA * x_ref[...] + gate * proj


def ple_residual(x, xb, p, wg, wp, li):
    m, d = x.shape
    pd = p.shape[2]
    tm = min(512, m)
    tn = min(1024, d)
    return pl.pallas_call(
        _ple_kernel,
        grid=(m // tm, d // tn),
        in_specs=[
            pl.BlockSpec((tm, tn), lambda i, j: (i, j)),
            pl.BlockSpec((tm, d), lambda i, j: (i, 0)),
            pl.BlockSpec((None, tm, pd), lambda i, j: (li, i, 0)),
            pl.BlockSpec((None, d, tn), lambda i, j: (li, 0, j)),
            pl.BlockSpec((None, pd, tn), lambda i, j: (li, 0, j)),
        ],
        out_specs=pl.BlockSpec((tm, tn), lambda i, j: (i, j)),
        out_shape=jax.ShapeDtypeStruct((m, d), F32),
        compiler_params=_cparams(("parallel", "arbitrary")),
        name="ple_residual",
    )(x, xb, p, wg, wp)


def _layer_norm(y, g, b):
    mu = jnp.mean(y, axis=-1, keepdims=True)
    yc = y - mu
    var = jnp.mean(yc * yc, axis=-1, keepdims=True)
    return yc * lax.rsqrt(var + LN_EPS) * g + b


def _ln_kernel(y_ref, g_ref, b_ref, of_ref, ob_ref):
    out = _layer_norm(y_ref[...], g_ref[...], b_ref[...])
    of_ref[...] = out
    ob_ref[...] = out.astype(BF16)


LN_ROWS = 256


def layernorm(y, g, b):
    m, d = y.shape
    tm = min(LN_ROWS, m)
    row = pl.BlockSpec((tm, d), lambda i: (i, 0))
    vec = pl.BlockSpec((1, d), lambda i: (0, 0))
    return pl.pallas_call(
        _ln_kernel,
        grid=(m // tm,),
        in_specs=[row, vec, vec],
        out_specs=[row, row],
        out_shape=[jax.ShapeDtypeStruct((m, d), F32), jax.ShapeDtypeStruct((m, d), BF16)],
        compiler_params=_cparams(("parallel",)),
        name="layernorm",
    )(y, g.reshape(1, d), b.reshape(1, d))


def _ln_split_kernel(y_ref, g_ref, b_ref, o0_ref, o1_ref, *, n0):
    i = pl.program_id(0)
    out = _layer_norm(y_ref[...], g_ref[...], b_ref[...])

    @pl.when(i < n0)
    def _():
        o0_ref[...] = out

    @pl.when(i >= n0)
    def _():
        o1_ref[...] = out


def layernorm_split(y, g, b, m0):
    m, d = y.shape
    tm = min(LN_ROWS, m0, m - m0)
    assert m0 % tm == 0 and (m - m0) % tm == 0
    n0 = m0 // tm
    vec = pl.BlockSpec((1, d), lambda i: (0, 0))
    return pl.pallas_call(
        functools.partial(_ln_split_kernel, n0=n0),
        grid=(m // tm,),
        in_specs=[pl.BlockSpec((tm, d), lambda i: (i, 0)), vec, vec],
        out_specs=[pl.BlockSpec((tm, d), lambda i: (jnp.minimum(i, n0 - 1), 0)),
                   pl.BlockSpec((tm, d), lambda i: (jnp.maximum(i - n0, 0), 0))],
        out_shape=[jax.ShapeDtypeStruct((m0, d), F32), jax.ShapeDtypeStruct((m - m0, d), F32)],
        compiler_params=_cparams(("arbitrary",)),
        name="layernorm_split",
    )(y, g.reshape(1, d), b.reshape(1, d))


def _merge_kernel(x0_ref, x1_ref, of_ref, ob_ref, *, n0):
    x = jnp.where(pl.program_id(0) < n0, x0_ref[...], x1_ref[...])
    of_ref[...] = x
    ob_ref[...] = x.astype(BF16)


def merge_streams(x0, x1):
    m0, d = x0.shape
    m1 = x1.shape[0]
    tm = min(LN_ROWS, m0, m1)
    assert m0 % tm == 0 and m1 % tm == 0
    n0 = m0 // tm
    row = pl.BlockSpec((tm, d), lambda i: (i, 0))
    return pl.pallas_call(
        functools.partial(_merge_kernel, n0=n0),
        grid=((m0 + m1) // tm,),
        in_specs=[pl.BlockSpec((tm, d), lambda i: (jnp.minimum(i, n0 - 1), 0)),
                  pl.BlockSpec((tm, d), lambda i: (jnp.maximum(i - n0, 0), 0))],
        out_specs=[row, row],
        out_shape=[jax.ShapeDtypeStruct((m0 + m1, d), F32), jax.ShapeDtypeStruct((m0 + m1, d), BF16)],
        compiler_params=_cparams(("parallel",)),
        name="merge_streams",
    )(x0, x1)


def _t5_bucket(rel):
    half = REL_BUCKETS // 2
    max_exact = half // 2
    n = jnp.abs(rel)
    side = jnp.where(rel > 0, half, 0)
    nf = jnp.maximum(n, 1).astype(F32)
    large = max_exact + (jnp.log(nf / max_exact) / math.log(REL_MAX_DIST / max_exact)
                         * (half - max_exact)).astype(jnp.int32)
    large = jnp.minimum(large, half - 1)
    return side + jnp.where(n < max_exact, n, large)


DIFF_TILE = 256
DIFF_ONES_ROWS = 16
DIFF_GROUP_BY_TILES = {32: 2, 16: 8}
DIFF_GROUP_DEFAULT = 4
SWA_TQ = 512
LOG2E = math.log2(math.e)


def _bucket_select(bucket, table_ref, col, init):
    out = jnp.full(bucket.shape, init, F32)
    for b in range(REL_BUCKETS):
        out = jnp.where(bucket == b, table_ref[b, col], out)
    return out


NA_QROWS = 8
NA_QBLK = NA_QROWS * GRID_W
NA_KBLK = NA_QBLK // 2
NA_RPB_C = 2 * NA_WIN_C - 1


NA_SPAN_ROWS = 2 * NA_QROWS


def _na_build_bias(rpb_ref, tab_ref, hd):
    shape = (GRID_W, 2 * GRID_W)
    lane = lax.broadcasted_iota(jnp.int32, shape, 1)
    qc = lax.broadcasted_iota(jnp.int32, shape, 0)
    second = lane >= GRID_W
    kc = jnp.where(second, lane - GRID_W, lane)
    d = kc - qc + (NA_WIN_C - 1)
    c0 = jnp.clip(qc - NA_WIN_C // 2, 0, GRID_W - NA_WIN_C)
    col_ok = jnp.logical_and(kc >= c0, kc < c0 + NA_WIN_C)
    pairs = []
    for r in range(2 * NA_WIN_R - 2):
        t2 = jnp.full(shape, NEG_INF, F32)
        for dd in range(NA_RPB_C):
            val = jnp.where(second, rpb_ref[hd, (r + 1) * NA_RPB_C + dd], rpb_ref[hd, r * NA_RPB_C + dd])
            t2 = jnp.where(d == dd, val, t2)
        pairs.append(jnp.where(col_ok, t2, NEG_INF))
    masked = jnp.full(shape, NEG_INF, F32)
    half = NA_WIN_R // 2
    for variant in range(3):
        for il in range(NA_QROWS):
            back = (min(il, half), half, max(il, half))[variant]
            start = il + half - back
            a0 = NA_WIN_R - 1 - back
            for jl in range(0, NA_SPAN_ROWS, 2):
                ok0 = start <= jl < start + NA_WIN_R
                ok1 = start <= jl + 1 < start + NA_WIN_R
                r = a0 + jl - start
                assert not (ok0 or ok1) or 0 <= r < len(pairs)
                if ok0 and ok1:
                    tile = pairs[r]
                elif ok0:
                    tile = jnp.where(second, NEG_INF, pairs[r])
                elif ok1:
                    tile = jnp.where(second, pairs[r], NEG_INF)
                else:
                    tile = masked
                tab_ref[variant, il * GRID_W:(il + 1) * GRID_W, jl * GRID_W:(jl + 2) * GRID_W] = tile


def _na_kernel(kidx_ref, var_ref, q_ref, k0, k1, k2, k3, v0, v1, v2, v3, rpb_ref, mix_ref, o_ref, tab_ref):
    del mix_ref
    hd = pl.program_id(0)
    n = pl.program_id(1)
    scale = HEAD_DIM ** -0.5

    pl.when(n == 0)(functools.partial(_na_build_bias, rpb_ref, tab_ref, hd))

    kspan = jnp.concatenate([k0[...], k1[...], k2[...], k3[...]], axis=0)
    vspan = jnp.concatenate([v0[...], v1[...], v2[...], v3[...]], axis=0)
    s = lax.dot_general(q_ref[...], kspan, NT_DIMS, preferred_element_type=F32) * scale + tab_ref[var_ref[n]]
    m = jnp.max(s, axis=-1, keepdims=True)
    p = jnp.exp(s - m)
    l = jnp.sum(p, axis=-1, keepdims=True)
    o = jnp.dot(p.astype(BF16), vspan, preferred_element_type=F32) / l
    o_ref[...] = o.astype(o_ref.dtype)


def neighbourhood_attention(h, mix, rpb, seq_lens):
    m = h.shape[0]
    nblk = m // NA_QBLK
    kidx = np.zeros((4, nblk), np.int32)
    variant = np.ones((nblk,), np.int32)
    base = 0
    for t in seq_lens:
        nb = t // NA_QBLK
        assert nb >= 2 and t % NA_QBLK == 0
        lo, hi = 2 * base, 2 * (base + nb) - 1
        for n in range(nb):
            for j in range(4):
                kidx[j, base + n] = np.clip(2 * (base + n) - 1 + j, lo, hi)
        variant[base] = 0
        variant[base + nb - 1] = 2
        base += nb

    def kmap(j, col):
        return lambda hd, n, kidx_ref, var_ref: (kidx_ref[j, n], col + hd)

    kv_specs = [pl.BlockSpec((NA_KBLK, HEAD_DIM), kmap(j, COL_KA)) for j in range(4)]
    kv_specs += [pl.BlockSpec((NA_KBLK, HEAD_DIM), kmap(j, COL_VA)) for j in range(4)]
    grid_spec = pltpu.PrefetchScalarGridSpec(
        num_scalar_prefetch=2,
        grid=(NA_HEADS, nblk),
        in_specs=[pl.BlockSpec((NA_QBLK, HEAD_DIM), lambda hd, n, *_: (n, COL_QA + hd))]
        + kv_specs
        + [pl.BlockSpec(memory_space=pltpu.SMEM), pl.BlockSpec(memory_space=pl.ANY)],
        out_specs=pl.BlockSpec((NA_QBLK, HEAD_DIM), lambda hd, n, *_: (n, MIX_COL_A + hd)),
        scratch_shapes=[pltpu.VMEM((3, NA_QBLK, NA_SPAN_ROWS * GRID_W), F32)],
    )
    return pl.pallas_call(
        _na_kernel,
        grid_spec=grid_spec,
        out_shape=jax.ShapeDtypeStruct(mix.shape, mix.dtype),
        input_output_aliases={12: 0},
        compiler_params=_cparams(("arbitrary", "arbitrary")),
        name="neighbourhood_attention",
    )(jnp.asarray(kidx), jnp.asarray(variant), h, *([h] * 8),
      rpb.reshape(NA_HEADS, -1).astype(F32), mix)


def _diff_kernel(far_ref, rel_ref, q_ref, qn_ref, k_ref, v_ref, bkt_ref, lamv_ref, g_ref, mix_ref, o_ref,
                 s0_ref, s1_ref, p0_ref, p1_ref, vt_ref, bias_ref, m0_ref, *, t, nk, group, lambda_init):
    del mix_ref
    hd = pl.program_id(1)
    i = pl.program_id(2)
    kg = group * t
    ngroups = nk // group
    neg = jnp.full((1, t), NEG_INF, F32)

    def component(q, c):
        lane = lax.broadcasted_iota(jnp.int32, q.shape, 1)
        keep = lane >= DIFF_QK_DIM if c else lane < DIFF_QK_DIM
        return jnp.where(keep, q, jnp.zeros_like(q)) * jnp.asarray(DIFF_QK_DIM ** -0.5, BF16)

    c_lo = rel_ref[far_ref[0], hd] * LOG2E
    c_hi = rel_ref[far_ref[1], hd] * LOG2E

    def far_bias(j, near, tile):
        return jnp.where(j < tile - 1, c_lo, jnp.where(j > tile + 1, c_hi, near))

    def scores(qc, s_ref, g, m, tile):
        lo = g * kg
        s = lax.dot_general(k_ref[lo:lo + kg, :], qc, NT_DIMS, preferred_element_type=F32) * LOG2E
        s_ref[lo:lo + kg, :] = s
        for u in range(group):
            m = jnp.maximum(m, jnp.max(s[u * t:(u + 1) * t], axis=0, keepdims=True)
                            + far_bias(g * group + u, NEG_INF, tile))
        return m

    def add_diagonal_bias(s_ref, m, tile):
        for o in (-1, 0, 1):
            j = tile + o
            ok = jnp.logical_and(j >= 0, j < nk)
            row = pl.multiple_of(jnp.clip(j, 0, nk - 1) * t, t)
            blk = s_ref[pl.ds(row, t), :] + jnp.where(ok, bias_ref[o + 1], 0.0)
            s_ref[pl.ds(row, t), :] = blk
            m = jnp.maximum(m, jnp.max(blk, axis=0, keepdims=True) + jnp.where(ok, 0.0, NEG_INF))
        return m

    def exp_group(s_ref, p_ref, m, g):
        for jc in range(g * group, (g + 1) * group):
            p = jnp.exp2(s_ref[jc * t:(jc + 1) * t, :] - (m - far_bias(jc, 0.0, i)))
            p_ref[jc * t:(jc + 1) * t, :] = p.astype(BF16)

    @pl.when(i == 0)
    def _prepare_head():
        for o in range(3):
            bias_ref[o] = _bucket_select(bkt_ref[o], rel_ref, hd, 0.0) * LOG2E
        for c in range(nk):
            vt_ref[:HEAD_DIM, c * t:(c + 1) * t] = v_ref[c * t:(c + 1) * t, :].astype(F32).T.astype(BF16)
        vt_ref[HEAD_DIM:, :] = jnp.ones((DIFF_ONES_ROWS, nk * t), BF16)
        q0 = component(q_ref[...], 0)
        m = neg
        for g in range(ngroups):
            m = scores(q0, s0_ref, g, m, 0)
        m0_ref[...] = add_diagonal_bias(s0_ref, m, 0)

    m0 = m0_ref[...]
    q1 = component(q_ref[...], 1)
    m1 = neg
    for g in range(ngroups):
        m1 = scores(q1, s1_ref, g, m1, i)
        exp_group(s0_ref, p0_ref, m0, g)
    m1 = add_diagonal_bias(s1_ref, m1, i)
    qn0 = component(qn_ref[...], 0)
    pv0 = jnp.zeros((HEAD_DIM + DIFF_ONES_ROWS, t), F32)
    m0n = neg
    for g in range(ngroups):
        pv0 = pv0 + jnp.dot(vt_ref[:, g * kg:(g + 1) * kg], p0_ref[g * kg:(g + 1) * kg, :],
                            preferred_element_type=F32)
        exp_group(s1_ref, p1_ref, m1, g)
        m0n = scores(qn0, s0_ref, g, m0n, i + 1)
    m0_ref[...] = add_diagonal_bias(s0_ref, m0n, i + 1)
    half = (ngroups + 1) // 2 * kg
    pv1 = jnp.dot(vt_ref[:, :half], p1_ref[:half, :], preferred_element_type=F32)
    if half < nk * t:
        pv1 = pv1 + jnp.dot(vt_ref[:, half:], p1_ref[half:, :], preferred_element_type=F32)
    pv = [pv0, pv1]
    acc = [x[:HEAD_DIM] for x in pv]
    l = [x[HEAD_DIM:HEAD_DIM + 1] for x in pv]
    lamv = lamv_ref[...]
    lam = (jnp.exp(jnp.sum(lamv[0:1] * lamv[1:2], axis=-1, keepdims=True))
           - jnp.exp(jnp.sum(lamv[2:3] * lamv[3:4], axis=-1, keepdims=True)) + lambda_init)
    o_t = acc[0] / l[0] - lam * (acc[1] / l[1])
    ms = jnp.mean(o_t * o_t, axis=0, keepdims=True)
    o_t = o_t * lax.rsqrt(ms + RMS_EPS) * g_ref[...] * (1.0 - lambda_init)
    o_ref[...] = o_t.T.astype(o_ref.dtype)


def differential_attention(h, mix, row0, nb, t_seq, rel_bias, lamv, sub_g, lambda_init):
    t = DIFF_TILE
    nq = t_seq // t
    group = DIFF_GROUP_BY_TILES.get(nq, min(DIFF_GROUP_DEFAULT, nq))
    assert row0 % t_seq == 0 and t_seq % t == 0 and nq % group == 0
    qb0, sb0 = row0 // t, row0 // t_seq
    kq = np.arange(t)[:, None] - np.arange(t)[None, :]
    bkt = _t5_bucket(jnp.asarray(np.stack([kq + o * t for o in (-1, 0, 1)]), dtype=jnp.int32))
    far = _t5_bucket(jnp.asarray([-2 * t, 2 * t], dtype=jnp.int32))
    smem = pl.BlockSpec(memory_space=pltpu.SMEM)
    return pl.pallas_call(
        functools.partial(_diff_kernel, t=t, nk=nq, group=group, lambda_init=lambda_init),
        grid=(nb, DIFF_HEADS, nq),
        in_specs=[
            smem, smem,
            pl.BlockSpec((t, HEAD_DIM), lambda b, hd, i: (qb0 + b * nq + i, COL_QB + hd)),
            pl.BlockSpec((t, HEAD_DIM),
                         lambda b, hd, i: (qb0 + b * nq + jnp.minimum(i + 1, nq - 1), COL_QB + hd)),
            pl.BlockSpec((t_seq, HEAD_DIM), lambda b, hd, i: (sb0 + b, COL_KB + hd)),
            pl.BlockSpec((t_seq, HEAD_DIM), lambda b, hd, i: (sb0 + b, COL_VB + hd)),
            pl.BlockSpec((3, t, t), lambda b, hd, i: (0, 0, 0)),
            pl.BlockSpec((4, DIFF_QK_DIM), lambda b, hd, i: (0, 0)),
            pl.BlockSpec((HEAD_DIM, 1), lambda b, hd, i: (0, 0)),
            pl.BlockSpec(memory_space=pl.ANY),
        ],
        out_specs=pl.BlockSpec((t, HEAD_DIM), lambda b, hd, i: (qb0 + b * nq + i, MIX_COL_B + hd)),
        scratch_shapes=[
            pltpu.VMEM((t_seq, t), F32), pltpu.VMEM((t_seq, t), F32),
            pltpu.VMEM((t_seq, t), BF16), pltpu.VMEM((t_seq, t), BF16),
            pltpu.VMEM((HEAD_DIM + DIFF_ONES_ROWS, t_seq), BF16),
            pltpu.VMEM((3, t, t), F32),
            pltpu.VMEM((1, t), F32),
        ],
        out_shape=jax.ShapeDtypeStruct(mix.shape, mix.dtype),
        input_output_aliases={9: 0},
        compiler_params=_cparams(("arbitrary", "arbitrary", "arbitrary")),
        name="differential_attention",
    )(far, rel_bias.astype(F32), h, h, h, h, bkt, lamv, sub_g.reshape(HEAD_DIM, 1).astype(F32), mix)


def _swa_kernel(edge_ref, q0, q1, q2, kp, kc, kn, vp, vc, vn, bkt_ref, rel_ref, sink_ref, mix_ref, o_ref,
                bias_ref):
    del mix_ref
    i = pl.program_id(0)
    j = pl.program_id(1)
    scale = HEAD_DIM ** -0.5
    w = SWA_TQ + 2 * SWA_WINDOW

    @pl.when(i == 0)
    def _build_bias():
        for g in range(SWA_GROUP):
            head = j * SWA_GROUP + g
            bias_ref[head] = _bucket_select(bkt_ref[...], rel_ref, DIFF_HEADS + head, NEG_INF)

    kwin = jnp.concatenate([kp[...], kc[...], kn[...]], axis=0)
    vwin = jnp.concatenate([vp[...], vc[...], vn[...]], axis=0)
    col = lax.broadcasted_iota(jnp.int32, (1, w), 1)
    is_first = edge_ref[0, i] == 1
    is_last = edge_ref[1, i] == 1
    valid = jnp.logical_and(jnp.logical_or(col >= SWA_WINDOW, jnp.logical_not(is_first)),
                            jnp.logical_or(col < SWA_TQ + SWA_WINDOW, jnp.logical_not(is_last)))
    for g, q_ref in enumerate((q0, q1, q2)):
        head = j * SWA_GROUP + g
        sink = sink_ref[head]
        s = lax.dot_general(q_ref[...], kwin, NT_DIMS, preferred_element_type=F32) * scale + bias_ref[head]
        s = jnp.where(valid, s, NEG_INF)
        m = jnp.maximum(jnp.max(s, axis=-1, keepdims=True), sink)
        p = jnp.exp(s - m)
        l = jnp.sum(p, axis=-1, keepdims=True) + jnp.exp(sink - m)
        o = jnp.dot(p.astype(BF16), vwin, preferred_element_type=F32) / l
        o_ref[:, g * HEAD_DIM:(g + 1) * HEAD_DIM] = o.astype(o_ref.dtype)


def sliding_window_attention(h, mix, rel_bias, sink, seq_lens):
    m = h.shape[0]
    nblk = m // SWA_TQ
    ratio = SWA_TQ // SWA_WINDOW
    edge = np.zeros((2, nblk), np.int32)
    base = 0
    for t in seq_lens:
        assert t % SWA_TQ == 0
        edge[0, base] = 1
        base += t // SWA_TQ
        edge[1, base - 1] = 1
    last_halo = m // SWA_WINDOW - 1
    w = SWA_TQ + 2 * SWA_WINDOW
    rel = (np.arange(w)[None, :] - SWA_WINDOW) - np.arange(SWA_TQ)[:, None]
    bkt = jnp.where(jnp.asarray(np.abs(rel) <= SWA_WINDOW), _t5_bucket(jnp.asarray(rel, dtype=jnp.int32)), -1)

    def qmap(g):
        return lambda i, j, *_: (i, COL_QC + j * SWA_GROUP + g)

    def halo(col):
        return [
            pl.BlockSpec((SWA_WINDOW, HEAD_DIM), lambda i, j, *_: (jnp.maximum(ratio * i - 1, 0), col + j)),
            pl.BlockSpec((SWA_TQ, HEAD_DIM), lambda i, j, *_: (i, col + j)),
            pl.BlockSpec((SWA_WINDOW, HEAD_DIM),
                         lambda i, j, *_: (jnp.minimum(ratio * (i + 1), last_halo), col + j)),
        ]

    smem = pl.BlockSpec(memory_space=pltpu.SMEM)
    grid_spec = pltpu.PrefetchScalarGridSpec(
        num_scalar_prefetch=1,
        grid=(nblk, SWA_KV_HEADS),
        in_specs=[pl.BlockSpec((SWA_TQ, HEAD_DIM), qmap(g)) for g in range(SWA_GROUP)]
        + halo(COL_KC) + halo(COL_VC)
        + [pl.BlockSpec((SWA_TQ, w), lambda i, j, *_: (0, 0)), smem, smem,
           pl.BlockSpec(memory_space=pl.ANY)],
        out_specs=pl.BlockSpec((SWA_TQ, SWA_GROUP * HEAD_DIM),
                               lambda i, j, *_: (i, MIX_COL_C // SWA_GROUP + j)),
        scratch_shapes=[pltpu.VMEM((SWA_Q_HEADS, SWA_TQ, w), F32)],
    )
    assert MIX_COL_C % SWA_GROUP == 0
    return pl.pallas_call(
        _swa_kernel,
        grid_spec=grid_spec,
        out_shape=jax.ShapeDtypeStruct(mix.shape, mix.dtype),
        input_output_aliases={13: 0},
        compiler_params=_cparams(("arbitrary", "arbitrary")),
        name="sliding_window_attention",
    )(jnp.asarray(edge), *([h] * 9), bkt, rel_bias.astype(F32), sink.astype(F32), mix)


def encoder_trunk(x0, x1, p, seq_groups, w_in, w_out, na_rpb, lam_q1, lam_k1, lam_q2, lam_k2,
                  diff_g, sink, rel_bias, ffn1_gate, ffn1_up, ffn1_down,
                  ffn2_gate, ffn2_up, ffn2_down, ple_gate, ple_proj, ln_g, ln_b):
    x, xb = merge_streams(x0, x1)
    m = x.shape[0]
    seq_lens = [t for _, nb, t in seq_groups for _ in range(nb)]
    pb = p.astype(BF16)
    w_in, ple_gate, ple_proj = (cast_weights(w) for w in (w_in, ple_gate, ple_proj))
    w_out = cast_weights(w_out, row_rotate=A_W + B_W)
    ffn1_gate, ffn1_up, ffn2_gate, ffn2_up = (
        cast_weights(w, cols_out=D_FF_PAD) for w in (ffn1_gate, ffn1_up, ffn2_gate, ffn2_up))
    ffn1_down, ffn2_down = (cast_weights(w, rows_out=D_FF_PAD) for w in (ffn1_down, ffn2_down))
    ffn_tk = D_FF_PAD // 4
    for li in range(DEPTH):
        hid = gateup(xb, ffn1_gate, ffn1_up, li)
        y = matmul(hid, ffn1_down, li, F32, tk=ffn_tk, res=x, res_scale=DEEPNORM_ALPHA, acc_scale=0.5)
        x, xb = layernorm(y, ln_g[li, 0], ln_b[li, 0])
        h = matmul(xb, w_in, li, BF16)
        lambda_init = 0.8 - 0.6 * math.exp(-0.3 * li)
        lamv = jnp.stack([lam_q1[li], lam_k1[li], lam_q2[li], lam_k2[li]]).astype(F32)
        mix = jnp.zeros((m, MIX_W), BF16)
        mix = neighbourhood_attention(h, mix, na_rpb[li], seq_lens)
        for row0, nb, t in seq_groups:
            mix = differential_attention(h, mix, row0, nb, t, rel_bias, lamv, diff_g[li], lambda_init)
        mix = sliding_window_attention(h, mix, rel_bias, sink[li], seq_lens)
        y = matmul(mix, w_out, li, F32, res=x, res_scale=DEEPNORM_ALPHA)
        x, xb = layernorm(y, ln_g[li, 1], ln_b[li, 1])
        res = ple_residual(x, xb, pb, ple_gate, ple_proj, li)
        hid = gateup(xb, ffn2_gate, ffn2_up, li)
        y = matmul(hid, ffn2_down, li, F32, tk=ffn_tk, res=res, res_scale=1.0, acc_scale=0.5)
        if li < DEPTH - 1:
            x, xb = layernorm(y, ln_g[li, 2], ln_b[li, 2])
    return layernorm_split(y, ln_g[DEPTH - 1, 2], ln_b[DEPTH - 1, 2], x0.shape[0])


def kernel(x_prompt, x_sample, p_prompt, p_sample, w_in, w_out, na_rpb, lam_q1, lam_k1, lam_q2, lam_k2,
           diff_g, sink, rel_bias, ffn1_gate, ffn1_up, ffn1_down, ffn2_gate, ffn2_up, ffn2_down,
           ple_gate, ple_proj, ln_g, ln_b):
    bp, tp, d = x_prompt.shape
    bs, ts, _ = x_sample.shape
    mp, ms = bp * tp, bs * ts
    p = jnp.concatenate([p_prompt.reshape(DEPTH, mp, PLE_DIM), p_sample.reshape(DEPTH, ms, PLE_DIM)], axis=1)
    seq_groups = [(0, bp, tp), (mp, bs, ts)]
    y0, y1 = encoder_trunk(x_prompt.reshape(mp, d), x_sample.reshape(ms, d), p, seq_groups,
                           w_in, w_out, na_rpb, lam_q1, lam_k1, lam_q2, lam_k2,
                           diff_g, sink, rel_bias, ffn1_gate, ffn1_up, ffn1_down,
                           ffn2_gate, ffn2_up, ffn2_down, ple_gate, ple_proj, ln_g, ln_b)
    return y0.reshape(bp, tp, d), y1.reshape(bs, ts, d)
```

```python
import functools
import math

import jax
import jax.numpy as jnp
import numpy as np
from jax import lax
from jax.experimental import pallas as pl
from jax.experimental.pallas import tpu as pltpu

F32 = jnp.float32
BF16 = jnp.bfloat16

DEPTH = 2
GRID_W = 64
HEAD_DIM = 128
NA_HEADS = 8
NA_WIN_R = 8
NA_WIN_C = 16
DIFF_HEADS = 12
DIFF_QK_DIM = 64
SWA_Q_HEADS = 12
SWA_KV_HEADS = 4
SWA_GROUP = SWA_Q_HEADS // SWA_KV_HEADS
SWA_WINDOW = 128
REL_BUCKETS = 32
REL_MAX_DIST = 128
D_FF = 11008
PLE_DIM = 256
LN_EPS = 1e-5
RMS_EPS = 1e-5
NEG_INF = -1e30
DEEPNORM_ALPHA = (2 * DEPTH) ** 0.25

A_W = NA_HEADS * HEAD_DIM
B_W = DIFF_HEADS * HEAD_DIM
C_Q_W = SWA_Q_HEADS * HEAD_DIM
C_KV_W = SWA_KV_HEADS * HEAD_DIM
MIX_W = A_W + B_W + C_Q_W
IN_W = 3 * A_W + 3 * B_W + C_Q_W + 2 * C_KV_W
COL_QA, COL_KA, COL_VA = 0, A_W // 128, 2 * A_W // 128
COL_QB = 3 * A_W // 128
COL_KB = COL_QB + B_W // 128
COL_VB = COL_KB + B_W // 128
COL_QC = COL_VB + B_W // 128
COL_KC = COL_QC + C_Q_W // 128
COL_VC = COL_KC + C_KV_W // 128
MIX_COL_C, MIX_COL_A, MIX_COL_B = 0, C_Q_W // 128, (C_Q_W + A_W) // 128

V7X_VMEM_BYTES = 64 * 1024 * 1024
VMEM_LIMIT = V7X_VMEM_BYTES - 8 * 1024 * 1024

FF_TILE = 512
D_FF_PAD = -(-D_FF // 1024) * 1024

NT_DIMS = (((1,), (1,)), ((), ()))


def _cparams(sem):
    return pltpu.CompilerParams(dimension_semantics=sem, vmem_limit_bytes=VMEM_LIMIT)


def _cast_kernel(w_ref, o_ref, *, rows, cols, masked):
    x = w_ref[0]
    if masked:
        tr, tc = x.shape
        r = pl.program_id(1) * tr + lax.broadcasted_iota(jnp.int32, x.shape, 0)
        c = pl.program_id(2) * tc + lax.broadcasted_iota(jnp.int32, x.shape, 1)
        x = jnp.where(jnp.logical_and(r < rows, c < cols), x, 0.0)
    o_ref[0] = x.astype(o_ref.dtype)


def cast_weights(w, rows_out=None, cols_out=None, row_rotate=0):
    nl, rows, cols = w.shape
    rows_out = rows if rows_out is None else rows_out
    cols_out = cols if cols_out is None else cols_out
    tr = math.gcd(math.gcd(rows_out, 1024), row_rotate)
    tc = math.gcd(cols_out, 2048)
    nrb = rows_out // tr
    assert row_rotate == 0 or rows_out == rows
    shift = row_rotate // tr
    spec = pl.BlockSpec((1, tr, tc), lambda l, i, j: (l, i, j))
    return pl.pallas_call(
        functools.partial(_cast_kernel, rows=rows, cols=cols,
                          masked=(rows_out != rows or cols_out != cols)),
        grid=(nl, nrb, cols_out // tc),
        in_specs=[pl.BlockSpec((1, tr, tc), lambda l, i, j: (l, (i + shift) % nrb, j))],
        out_specs=spec,
        out_shape=jax.ShapeDtypeStruct((nl, rows_out, cols_out), BF16),
        compiler_params=_cparams(("parallel", "parallel", "parallel")),
        name="cast_weights",
    )(w)


def _gateup_kernel(x_ref, wg_ref, wu_ref, o_ref):
    x = x_ref[...]
    g = jnp.dot(x, wg_ref[...], preferred_element_type=F32)
    u = jnp.dot(x, wu_ref[...], preferred_element_type=F32)
    o_ref[...] = (g * jax.nn.sigmoid(g) * u).astype(o_ref.dtype)


def gateup(x, wg, wu, li):
    m, k = x.shape
    f = wg.shape[2]
    tm = min(1024, m)
    tn = min(FF_TILE, f)
    return pl.pallas_call(
        _gateup_kernel,
        grid=(m // tm, f // tn),
        in_specs=[
            pl.BlockSpec((tm, k), lambda i, j: (i, 0)),
            pl.BlockSpec((None, k, tn), lambda i, j: (li, 0, j)),
            pl.BlockSpec((None, k, tn), lambda i, j: (li, 0, j)),
        ],
        out_specs=pl.BlockSpec((tm, tn), lambda i, j: (i, j)),
        out_shape=jax.ShapeDtypeStruct((m, f), BF16),
        compiler_params=_cparams(("parallel", "arbitrary")),
        name="gateup",
    )(x, wg, wu)


def _matmul_kernel(*refs, nk, res_scale, acc_scale):
    if res_scale is None:
        (a_ref, w_ref, o_ref, acc_ref), res_ref = refs, None
    else:
        a_ref, w_ref, res_ref, o_ref, acc_ref = refs
    kk = pl.program_id(2)

    def finish(total):
        if res_ref is not None:
            total = res_scale * res_ref[...] + acc_scale * total
        o_ref[...] = total.astype(o_ref.dtype)

    if nk == 1:
        finish(jnp.dot(a_ref[...], w_ref[...], preferred_element_type=F32))
        return

    @pl.when(kk == 0)
    def _():
        acc_ref[...] = jnp.zeros_like(acc_ref)

    acc_ref[...] += jnp.dot(a_ref[...], w_ref[...], preferred_element_type=F32)

    @pl.when(kk == nk - 1)
    def _():
        finish(acc_ref[...])


def matmul(a, w, li, out_dtype, tk=None, res=None, res_scale=None, acc_scale=1.0):
    m, k = a.shape
    n = w.shape[2]
    tm = min(1024, m)
    tn = min(1024, n)
    tk = k if tk is None else tk
    nk = k // tk
    in_specs = [
        pl.BlockSpec((tm, tk), lambda i, j, kk: (i, kk)),
        pl.BlockSpec((None, tk, tn), lambda i, j, kk: (li, kk, j)),
    ]
    operands = [a, w]
    if res is not None:
        in_specs.append(pl.BlockSpec((tm, tn), lambda i, j, kk: (i, j)))
        operands.append(res)
    return pl.pallas_call(
        functools.partial(_matmul_kernel, nk=nk, res_scale=res_scale if res is not None else None,
                          acc_scale=acc_scale),
        grid=(m // tm, n // tn, nk),
        in_specs=in_specs,
        out_specs=pl.BlockSpec((tm, tn), lambda i, j, kk: (i, j)),
        out_shape=jax.ShapeDtypeStruct((m, n), out_dtype),
        scratch_shapes=[pltpu.VMEM((tm, tn) if nk > 1 else (8, 128), F32)],
        compiler_params=_cparams(("parallel", "parallel", "arbitrary")),
        name="matmul",
    )(*operands)


def _ple_kernel(x_ref, xb_ref, p_ref, wg_ref, wp_ref, o_ref):
    gate = jax.nn.sigmoid(jnp.dot(xb_ref[...], wg_ref[...], preferred_element_type=F32))
    proj = jnp.dot(p_ref[...], wp_ref[...], preferred_element_type=F32)
    o_ref[...] = DEEPNORM_ALPHA * x_ref[...] + gate * proj


def ple_residual(x, xb, p, wg, wp, li):
    m, d = x.shape
    pd = p.shape[2]
    tm = min(512, m)
    tn = min(1024, d)
    return pl.pallas_call(
        _ple_kernel,
        grid=(m // tm, d // tn),
        in_specs=[
            pl.BlockSpec((tm, tn), lambda i, j: (i, j)),
            pl.BlockSpec((tm, d), lambda i, j: (i, 0)),
            pl.BlockSpec((None, tm, pd), lambda i, j: (li, i, 0)),
            pl.BlockSpec((None, d, tn), lambda i, j: (li, 0, j)),
            pl.BlockSpec((None, pd, tn), lambda i, j: (li, 0, j)),
        ],
        out_specs=pl.BlockSpec((tm, tn), lambda i, j: (i, j)),
        out_shape=jax.ShapeDtypeStruct((m, d), F32),
        compiler_params=_cparams(("parallel", "arbitrary")),
        name="ple_residual",
    )(x, xb, p, wg, wp)


def _layer_norm(y, g, b):
    mu = jnp.mean(y, axis=-1, keepdims=True)
    yc = y - mu
    var = jnp.mean(yc * yc, axis=-1, keepdims=True)
    return yc * lax.rsqrt(var + LN_EPS) * g + b


def _ln_kernel(y_ref, g_ref, b_ref, of_ref, ob_ref):
    out = _layer_norm(y_ref[...], g_ref[...], b_ref[...])
    of_ref[...] = out
    ob_ref[...] = out.astype(BF16)


LN_ROWS = 256


def layernorm(y, g, b):
    m, d = y.shape
    tm = min(LN_ROWS, m)
    row = pl.BlockSpec((tm, d), lambda i: (i, 0))
    vec = pl.BlockSpec((1, d), lambda i: (0, 0))
    return pl.pallas_call(
        _ln_kernel,
        grid=(m // tm,),
        in_specs=[row, vec, vec],
        out_specs=[row, row],
        out_shape=[jax.ShapeDtypeStruct((m, d), F32), jax.ShapeDtypeStruct((m, d), BF16)],
        compiler_params=_cparams(("parallel",)),
        name="layernorm",
    )(y, g.reshape(1, d), b.reshape(1, d))


def _ln_split_kernel(y_ref, g_ref, b_ref, o0_ref, o1_ref, *, n0):
    i = pl.program_id(0)
    out = _layer_norm(y_ref[...], g_ref[...], b_ref[...])

    @pl.when(i < n0)
    def _():
        o0_ref[...] = out

    @pl.when(i >= n0)
    def _():
        o1_ref[...] = out


def layernorm_split(y, g, b, m0):
    m, d = y.shape
    tm = min(LN_ROWS, m0, m - m0)
    assert m0 % tm == 0 and (m - m0) % tm == 0
    n0 = m0 // tm
    vec = pl.BlockSpec((1, d), lambda i: (0, 0))
    return pl.pallas_call(
        functools.partial(_ln_split_kernel, n0=n0),
        grid=(m // tm,),
        in_specs=[pl.BlockSpec((tm, d), lambda i: (i, 0)), vec, vec],
        out_specs=[pl.BlockSpec((tm, d), lambda i: (jnp.minimum(i, n0 - 1), 0)),
                   pl.BlockSpec((tm, d), lambda i: (jnp.maximum(i - n0, 0), 0))],
        out_shape=[jax.ShapeDtypeStruct((m0, d), F32), jax.ShapeDtypeStruct((m - m0, d), F32)],
        compiler_params=_cparams(("arbitrary",)),
        name="layernorm_split",
    )(y, g.reshape(1, d), b.reshape(1, d))


def _merge_kernel(x0_ref, x1_ref, of_ref, ob_ref, *, n0):
    x = jnp.where(pl.program_id(0) < n0, x0_ref[...], x1_ref[...])
    of_ref[...] = x
    ob_ref[...] = x.astype(BF16)


def merge_streams(x0, x1):
    m0, d = x0.shape
    m1 = x1.shape[0]
    tm = min(LN_ROWS, m0, m1)
    assert m0 % tm == 0 and m1 % tm == 0
    n0 = m0 // tm
    row = pl.BlockSpec((tm, d), lambda i: (i, 0))
    return pl.pallas_call(
        functools.partial(_merge_kernel, n0=n0),
        grid=((m0 + m1) // tm,),
        in_specs=[pl.BlockSpec((tm, d), lambda i: (jnp.minimum(i, n0 - 1), 0)),
                  pl.BlockSpec((tm, d), lambda i: (jnp.maximum(i - n0, 0), 0))],
        out_specs=[row, row],
        out_shape=[jax.ShapeDtypeStruct((m0 + m1, d), F32), jax.ShapeDtypeStruct((m0 + m1, d), BF16)],
        compiler_params=_cparams(("parallel",)),
        name="merge_streams",
    )(x0, x1)


def _t5_bucket(rel):
    half = REL_BUCKETS // 2
    max_exact = half // 2
    n = jnp.abs(rel)
    side = jnp.where(rel > 0, half, 0)
    nf = jnp.maximum(n, 1).astype(F32)
    large = max_exact + (jnp.log(nf / max_exact) / math.log(REL_MAX_DIST / max_exact)
                         * (half - max_exact)).astype(jnp.int32)
    large = jnp.minimum(large, half - 1)
    return side + jnp.where(n < max_exact, n, large)


DIFF_TILE = 256
DIFF_ONES_ROWS = 16
DIFF_GROUP_BY_TILES = {32: 2, 16: 8}
DIFF_GROUP_DEFAULT = 4
SWA_TQ = 512
LOG2E = math.log2(math.e)


def _bucket_select(bucket, table_ref, col, init):
    out = jnp.full(bucket.shape, init, F32)
    for b in range(REL_BUCKETS):
        out = jnp.where(bucket == b, table_ref[b, col], out)
    return out


NA_QROWS = 8
NA_QBLK = NA_QROWS * GRID_W
NA_KBLK = NA_QBLK // 2
NA_RPB_C = 2 * NA_WIN_C - 1


NA_SPAN_ROWS = 2 * NA_QROWS


def _na_build_bias(rpb_ref, tab_ref, hd):
    shape = (GRID_W, 2 * GRID_W)
    lane = lax.broadcasted_iota(jnp.int32, shape, 1)
    qc = lax.broadcasted_iota(jnp.int32, shape, 0)
    second = lane >= GRID_W
    kc = jnp.where(second, lane - GRID_W, lane)
    d = kc - qc + (NA_WIN_C - 1)
    c0 = jnp.clip(qc - NA_WIN_C // 2, 0, GRID_W - NA_WIN_C)
    col_ok = jnp.logical_and(kc >= c0, kc < c0 + NA_WIN_C)
    pairs = []
    for r in range(2 * NA_WIN_R - 2):
        t2 = jnp.full(shape, NEG_INF, F32)
        for dd in range(NA_RPB_C):
            val = jnp.where(second, rpb_ref[hd, (r + 1) * NA_RPB_C + dd], rpb_ref[hd, r * NA_RPB_C + dd])
            t2 = jnp.where(d == dd, val, t2)
        pairs.append(jnp.where(col_ok, t2, NEG_INF))
    masked = jnp.full(shape, NEG_INF, F32)
    half = NA_WIN_R // 2
    for variant in range(3):
        for il in range(NA_QROWS):
            back = (min(il, half), half, max(il, half))[variant]
            start = il + half - back
            a0 = NA_WIN_R - 1 - back
            for jl in range(0, NA_SPAN_ROWS, 2):
                ok0 = start <= jl < start + NA_WIN_R
                ok1 = start <= jl + 1 < start + NA_WIN_R
                r = a0 + jl - start
                assert not (ok0 or ok1) or 0 <= r < len(pairs)
                if ok0 and ok1:
                    tile = pairs[r]
                elif ok0:
                    tile = jnp.where(second, NEG_INF, pairs[r])
                elif ok1:
                    tile = jnp.where(second, pairs[r], NEG_INF)
                else:
                    tile = masked
                tab_ref[variant, il * GRID_W:(il + 1) * GRID_W, jl * GRID_W:(jl + 2) * GRID_W] = tile


def _na_kernel(kidx_ref, var_ref, q_ref, k0, k1, k2, k3, v0, v1, v2, v3, rpb_ref, mix_ref, o_ref, tab_ref):
    del mix_ref
    hd = pl.program_id(0)
    n = pl.program_id(1)
    scale = HEAD_DIM ** -0.5

    pl.when(n == 0)(functools.partial(_na_build_bias, rpb_ref, tab_ref, hd))

    kspan = jnp.concatenate([k0[...], k1[...], k2[...], k3[...]], axis=0)
    vspan = jnp.concatenate([v0[...], v1[...], v2[...], v3[...]], axis=0)
    s = lax.dot_general(q_ref[...], kspan, NT_DIMS, preferred_element_type=F32) * scale + tab_ref[var_ref[n]]
    m = jnp.max(s, axis=-1, keepdims=True)
    p = jnp.exp(s - m)
    l = jnp.sum(p, axis=-1, keepdims=True)
    o = jnp.dot(p.astype(BF16), vspan, preferred_element_type=F32) / l
    o_ref[...] = o.astype(o_ref.dtype)


def neighbourhood_attention(h, mix, rpb, seq_lens):
    m = h.shape[0]
    nblk = m // NA_QBLK
    kidx = np.zeros((4, nblk), np.int32)
    variant = np.ones((nblk,), np.int32)
    base = 0
    for t in seq_lens:
        nb = t // NA_QBLK
        assert nb >= 2 and t % NA_QBLK == 0
        lo, hi = 2 * base, 2 * (base + nb) - 1
        for n in range(nb):
            for j in range(4):
                kidx[j, base + n] = np.clip(2 * (base + n) - 1 + j, lo, hi)
        variant[base] = 0
        variant[base + nb - 1] = 2
        base += nb

    def kmap(j, col):
        return lambda hd, n, kidx_ref, var_ref: (kidx_ref[j, n], col + hd)

    kv_specs = [pl.BlockSpec((NA_KBLK, HEAD_DIM), kmap(j, COL_KA)) for j in range(4)]
    kv_specs += [pl.BlockSpec((NA_KBLK, HEAD_DIM), kmap(j, COL_VA)) for j in range(4)]
    grid_spec = pltpu.PrefetchScalarGridSpec(
        num_scalar_prefetch=2,
        grid=(NA_HEADS, nblk),
        in_specs=[pl.BlockSpec((NA_QBLK, HEAD_DIM), lambda hd, n, *_: (n, COL_QA + hd))]
        + kv_specs
        + [pl.BlockSpec(memory_space=pltpu.SMEM), pl.BlockSpec(memory_space=pl.ANY)],
        out_specs=pl.BlockSpec((NA_QBLK, HEAD_DIM), lambda hd, n, *_: (n, MIX_COL_A + hd)),
        scratch_shapes=[pltpu.VMEM((3, NA_QBLK, NA_SPAN_ROWS * GRID_W), F32)],
    )
    return pl.pallas_call(
        _na_kernel,
        grid_spec=grid_spec,
        out_shape=jax.ShapeDtypeStruct(mix.shape, mix.dtype),
        input_output_aliases={12: 0},
        compiler_params=_cparams(("arbitrary", "arbitrary")),
        name="neighbourhood_attention",
    )(jnp.asarray(kidx), jnp.asarray(variant), h, *([h] * 8),
      rpb.reshape(NA_HEADS, -1).astype(F32), mix)


def _diff_kernel(far_ref, rel_ref, q_ref, qn_ref, k_ref, v_ref, bkt_ref, lamv_ref, g_ref, mix_ref, o_ref,
                 s0_ref, s1_ref, p0_ref, p1_ref, vt_ref, bias_ref, m0_ref, *, t, nk, group, lambda_init):
    del mix_ref
    hd = pl.program_id(0)
    i = pl.program_id(2)
    kg = group * t
    ngroups = nk // group
    neg = jnp.full((1, t), NEG_INF, F32)

    def component(q, c):
        lane = lax.broadcasted_iota(jnp.int32, q.shape, 1)
        keep = lane >= DIFF_QK_DIM if c else lane < DIFF_QK_DIM
        return jnp.where(keep, q, jnp.zeros_like(q)) * jnp.asarray(DIFF_QK_DIM ** -0.5, BF16)

    c_lo = rel_ref[far_ref[0], hd] * LOG2E
    c_hi = rel_ref[far_ref[1], hd] * LOG2E

    def far_bias(j, near, tile):
        return jnp.where(j < tile - 1, c_lo, jnp.where(j > tile + 1, c_hi, near))

    def scores(qc, s_ref, g, m, tile):
        lo = g * kg
        s = lax.dot_general(k_ref[lo:lo + kg, :], qc, NT_DIMS, preferred_element_type=F32) * LOG2E
        s_ref[lo:lo + kg, :] = s
        for u in range(group):
            m = jnp.maximum(m, jnp.max(s[u * t:(u + 1) * t], axis=0, keepdims=True)
                            + far_bias(g * group + u, NEG_INF, tile))
        return m

    def add_diagonal_bias(s_ref, m, tile):
        for o in (-1, 0, 1):
            j = tile + o
            ok = jnp.logical_and(j >= 0, j < nk)
            row = pl.multiple_of(jnp.clip(j, 0, nk - 1) * t, t)
            blk = s_ref[pl.ds(row, t), :] + jnp.where(ok, bias_ref[o + 1], 0.0)
            s_ref[pl.ds(row, t), :] = blk
            m = jnp.maximum(m, jnp.max(blk, axis=0, keepdims=True) + jnp.where(ok, 0.0, NEG_INF))
        return m

    def exp_group(s_ref, p_ref, m, g):
        for jc in range(g * group, (g + 1) * group):
            p = jnp.exp2(s_ref[jc * t:(jc + 1) * t, :] - (m - far_bias(jc, 0.0, i)))
            p_ref[jc * t:(jc + 1) * t, :] = p.astype(BF16)

    @pl.when(jnp.logical_and(pl.program_id(1) == 0, i == 0))
    def _build_bias_tiles():
        for o in range(3):
            bias_ref[o] = _bucket_select(bkt_ref[o], rel_ref, hd, 0.0) * LOG2E

    @pl.when(i == 0)
    def _prepare_sequence():
        for c in range(nk):
            vt_ref[:HEAD_DIM, c * t:(c + 1) * t] = v_ref[c * t:(c + 1) * t, :].astype(F32).T.astype(BF16)
        vt_ref[HEAD_DIM:, :] = jnp.ones((DIFF_ONES_ROWS, nk * t), BF16)
        q0 = component(q_ref[...], 0)
        m = neg
        for g in range(ngroups):
            m = scores(q0, s0_ref, g, m, 0)
        m0_ref[...] = add_diagonal_bias(s0_ref, m, 0)

    m0 = m0_ref[...]
    q1 = component(q_ref[...], 1)
    m1 = neg
    for g in range(ngroups):
        m1 = scores(q1, s1_ref, g, m1, i)
        exp_group(s0_ref, p0_ref, m0, g)
    m1 = add_diagonal_bias(s1_ref, m1, i)
    qn0 = component(qn_ref[...], 0)
    pv0 = jnp.zeros((HEAD_DIM + DIFF_ONES_ROWS, t), F32)
    m0n = neg
    for g in range(ngroups):
        pv0 = pv0 + jnp.dot(vt_ref[:, g * kg:(g + 1) * kg], p0_ref[g * kg:(g + 1) * kg, :],
                            preferred_element_type=F32)
        exp_group(s1_ref, p1_ref, m1, g)
        m0n = scores(qn0, s0_ref, g, m0n, i + 1)
    m0_ref[...] = add_diagonal_bias(s0_ref, m0n, i + 1)
    half = (ngroups + 1) // 2 * kg
    pv1 = jnp.dot(vt_ref[:, :half], p1_ref[:half, :], preferred_element_type=F32)
    if half < nk * t:
        pv1 = pv1 + jnp.dot(vt_ref[:, half:], p1_ref[half:, :], preferred_element_type=F32)
    pv = [pv0, pv1]
    acc = [x[:HEAD_DIM] for x in pv]
    l = [x[HEAD_DIM:HEAD_DIM + 1] for x in pv]
    lamv = lamv_ref[...]
    lam = (jnp.exp(jnp.sum(lamv[0:1] * lamv[1:2], axis=-1, keepdims=True))
           - jnp.exp(jnp.sum(lamv[2:3] * lamv[3:4], axis=-1, keepdims=True)) + lambda_init)
    o_t = acc[0] / l[0] - lam * (acc[1] / l[1])
    ms = jnp.mean(o_t * o_t, axis=0, keepdims=True)
    o_t = o_t * lax.rsqrt(ms + RMS_EPS) * g_ref[...] * (1.0 - lambda_init)
    o_ref[...] = o_t.T.astype(o_ref.dtype)


def differential_attention(h, mix, row0, nb, t_seq, rel_bias, lamv, sub_g, lambda_init):
    t = DIFF_TILE
    nq = t_seq // t
    group = DIFF_GROUP_BY_TILES.get(nq, min(DIFF_GROUP_DEFAULT, nq))
    assert row0 % t_seq == 0 and t_seq % t == 0 and nq % group == 0
    qb0, sb0 = row0 // t, row0 // t_seq
    kq = np.arange(t)[:, None] - np.arange(t)[None, :]
    bkt = _t5_bucket(jnp.asarray(np.stack([kq + o * t for o in (-1, 0, 1)]), dtype=jnp.int32))
    far = _t5_bucket(jnp.asarray([-2 * t, 2 * t], dtype=jnp.int32))
    smem = pl.BlockSpec(memory_space=pltpu.SMEM)
    return pl.pallas_call(
        functools.partial(_diff_kernel, t=t, nk=nq, group=group, lambda_init=lambda_init),
        grid=(DIFF_HEADS, nb, nq),
        in_specs=[
            smem, smem,
            pl.BlockSpec((t, HEAD_DIM), lambda hd, b, i: (qb0 + b * nq + i, COL_QB + hd)),
            pl.BlockSpec((t, HEAD_DIM),
                         lambda hd, b, i: (qb0 + b * nq + jnp.minimum(i + 1, nq - 1), COL_QB + hd)),
            pl.BlockSpec((t_seq, HEAD_DIM), lambda hd, b, i: (sb0 + b, COL_KB + hd)),
            pl.BlockSpec((t_seq, HEAD_DIM), lambda hd, b, i: (sb0 + b, COL_VB + hd)),
            pl.BlockSpec((3, t, t), lambda hd, b, i: (0, 0, 0)),
            pl.BlockSpec((4, DIFF_QK_DIM), lambda hd, b, i: (0, 0)),
            pl.BlockSpec((HEAD_DIM, 1), lambda hd, b, i: (0, 0)),
            pl.BlockSpec(memory_space=pl.ANY),
        ],
        out_specs=pl.BlockSpec((t, HEAD_DIM), lambda hd, b, i: (qb0 + b * nq + i, MIX_COL_B + hd)),
        scratch_shapes=[
            pltpu.VMEM((t_seq, t), F32), pltpu.VMEM((t_seq, t), F32),
            pltpu.VMEM((t_seq, t), BF16), pltpu.VMEM((t_seq, t), BF16),
            pltpu.VMEM((HEAD_DIM + DIFF_ONES_ROWS, t_seq), BF16),
            pltpu.VMEM((3, t, t), F32),
            pltpu.VMEM((1, t), F32),
        ],
        out_shape=jax.ShapeDtypeStruct(mix.shape, mix.dtype),
        input_output_aliases={9: 0},
        compiler_params=_cparams(("arbitrary", "arbitrary", "arbitrary")),
        name="differential_attention",
    )(far, rel_bias.astype(F32), h, h, h, h, bkt, lamv, sub_g.reshape(HEAD_DIM, 1).astype(F32), mix)


def _swa_kernel(edge_ref, q0, q1, q2, kp, kc, kn, vp, vc, vn, bkt_ref, rel_ref, sink_ref, mix_ref, o_ref,
                bias_ref):
    del mix_ref
    i = pl.program_id(0)
    j = pl.program_id(1)
    scale = HEAD_DIM ** -0.5
    w = SWA_TQ + 2 * SWA_WINDOW

    @pl.when(i == 0)
    def _build_bias():
        for g in range(SWA_GROUP):
            head = j * SWA_GROUP + g
            bias_ref[head] = _bucket_select(bkt_ref[...], rel_ref, DIFF_HEADS + head, NEG_INF)

    kwin = jnp.concatenate([kp[...], kc[...], kn[...]], axis=0)
    vwin = jnp.concatenate([vp[...], vc[...], vn[...]], axis=0)
    col = lax.broadcasted_iota(jnp.int32, (1, w), 1)
    is_first = edge_ref[0, i] == 1
    is_last = edge_ref[1, i] == 1
    valid = jnp.logical_and(jnp.logical_or(col >= SWA_WINDOW, jnp.logical_not(is_first)),
                            jnp.logical_or(col < SWA_TQ + SWA_WINDOW, jnp.logical_not(is_last)))
    for g, q_ref in enumerate((q0, q1, q2)):
        head = j * SWA_GROUP + g
        sink = sink_ref[head]
        s = lax.dot_general(q_ref[...], kwin, NT_DIMS, preferred_element_type=F32) * scale + bias_ref[head]
        s = jnp.where(valid, s, NEG_INF)
        m = jnp.maximum(jnp.max(s, axis=-1, keepdims=True), sink)
        p = jnp.exp(s - m)
        l = jnp.sum(p, axis=-1, keepdims=True) + jnp.exp(sink - m)
        o = jnp.dot(p.astype(BF16), vwin, preferred_element_type=F32) / l
        o_ref[:, g * HEAD_DIM:(g + 1) * HEAD_DIM] = o.astype(o_ref.dtype)


def sliding_window_attention(h, mix, rel_bias, sink, seq_lens):
    m = h.shape[0]
    nblk = m // SWA_TQ
    ratio = SWA_TQ // SWA_WINDOW
    edge = np.zeros((2, nblk), np.int32)
    base = 0
    for t in seq_lens:
        assert t % SWA_TQ == 0
        edge[0, base] = 1
        base += t // SWA_TQ
        edge[1, base - 1] = 1
    last_halo = m // SWA_WINDOW - 1
    w = SWA_TQ + 2 * SWA_WINDOW
    rel = (np.arange(w)[None, :] - SWA_WINDOW) - np.arange(SWA_TQ)[:, None]
    bkt = jnp.where(jnp.asarray(np.abs(rel) <= SWA_WINDOW), _t5_bucket(jnp.asarray(rel, dtype=jnp.int32)), -1)

    def qmap(g):
        return lambda i, j, *_: (i, COL_QC + j * SWA_GROUP + g)

    def halo(col):
        return [
            pl.BlockSpec((SWA_WINDOW, HEAD_DIM), lambda i, j, *_: (jnp.maximum(ratio * i - 1, 0), col + j)),
            pl.BlockSpec((SWA_TQ, HEAD_DIM), lambda i, j, *_: (i, col + j)),
            pl.BlockSpec((SWA_WINDOW, HEAD_DIM),
                         lambda i, j, *_: (jnp.minimum(ratio * (i + 1), last_halo), col + j)),
        ]

    smem = pl.BlockSpec(memory_space=pltpu.SMEM)
    grid_spec = pltpu.PrefetchScalarGridSpec(
        num_scalar_prefetch=1,
        grid=(nblk, SWA_KV_HEADS),
        in_specs=[pl.BlockSpec((SWA_TQ, HEAD_DIM), qmap(g)) for g in range(SWA_GROUP)]
        + halo(COL_KC) + halo(COL_VC)
        + [pl.BlockSpec((SWA_TQ, w), lambda i, j, *_: (0, 0)), smem, smem,
           pl.BlockSpec(memory_space=pl.ANY)],
        out_specs=pl.BlockSpec((SWA_TQ, SWA_GROUP * HEAD_DIM),
                               lambda i, j, *_: (i, MIX_COL_C // SWA_GROUP + j)),
        scratch_shapes=[pltpu.VMEM((SWA_Q_HEADS, SWA_TQ, w), F32)],
    )
    assert MIX_COL_C % SWA_GROUP == 0
    return pl.pallas_call(
        _swa_kernel,
        grid_spec=grid_spec,
        out_shape=jax.ShapeDtypeStruct(mix.shape, mix.dtype),
        input_output_aliases={13: 0},
        compiler_params=_cparams(("arbitrary", "arbitrary")),
        name="sliding_window_attention",
    )(jnp.asarray(edge), *([h] * 9), bkt, rel_bias.astype(F32), sink.astype(F32), mix)


def encoder_trunk(x0, x1, p, seq_groups, w_in, w_out, na_rpb, lam_q1, lam_k1, lam_q2, lam_k2,
                  diff_g, sink, rel_bias, ffn1_gate, ffn1_up, ffn1_down,
                  ffn2_gate, ffn2_up, ffn2_down, ple_gate, ple_proj, ln_g, ln_b):
    x, xb = merge_streams(x0, x1)
    m = x.shape[0]
    seq_lens = [t for _, nb, t in seq_groups for _ in range(nb)]
    pb = p.astype(BF16)
    w_in, ple_gate, ple_proj = (cast_weights(w) for w in (w_in, ple_gate, ple_proj))
    w_out = cast_weights(w_out, row_rotate=A_W + B_W)
    ffn1_gate, ffn1_up, ffn2_gate, ffn2_up = (
        cast_weights(w, cols_out=D_FF_PAD) for w in (ffn1_gate, ffn1_up, ffn2_gate, ffn2_up))
    ffn1_down, ffn2_down = (cast_weights(w, rows_out=D_FF_PAD) for w in (ffn1_down, ffn2_down))
    ffn_tk = D_FF_PAD // 4
    for li in range(DEPTH):
        hid = gateup(xb, ffn1_gate, ffn1_up, li)
        y = matmul(hid, ffn1_down, li, F32, tk=ffn_tk, res=x, res_scale=DEEPNORM_ALPHA, acc_scale=0.5)
        x, xb = layernorm(y, ln_g[li, 0], ln_b[li, 0])
        h = matmul(xb, w_in, li, BF16)
        lambda_init = 0.8 - 0.6 * math.exp(-0.3 * li)
        lamv = jnp.stack([lam_q1[li], lam_k1[li], lam_q2[li], lam_k2[li]]).astype(F32)
        mix = jnp.zeros((m, MIX_W), BF16)
        mix = neighbourhood_attention(h, mix, na_rpb[li], seq_lens)
        for row0, nb, t in seq_groups:
            mix = differential_attention(h, mix, row0, nb, t, rel_bias, lamv, diff_g[li], lambda_init)
        mix = sliding_window_attention(h, mix, rel_bias, sink[li], seq_lens)
        y = matmul(mix, w_out, li, F32, res=x, res_scale=DEEPNORM_ALPHA)
        x, xb = layernorm(y, ln_g[li, 1], ln_b[li, 1])
        res = ple_residual(x, xb, pb, ple_gate, ple_proj, li)
        hid = gateup(xb, ffn2_gate, ffn2_up, li)
        y = matmul(hid, ffn2_down, li, F32, tk=ffn_tk, res=res, res_scale=1.0, acc_scale=0.5)
        if li < DEPTH - 1:
            x, xb = layernorm(y, ln_g[li, 2], ln_b[li, 2])
    return layernorm_split(y, ln_g[DEPTH - 1, 2], ln_b[DEPTH - 1, 2], x0.shape[0])


def kernel(x_prompt, x_sample, p_prompt, p_sample, w_in, w_out, na_rpb, lam_q1, lam_k1, lam_q2, lam_k2,
           diff_g, sink, rel_bias, ffn1_gate, ffn1_up, ffn1_down, ffn2_gate, ffn2_up, ffn2_down,
           ple_gate, ple_proj, ln_g, ln_b):
    bp, tp, d = x_prompt.shape
    bs, ts, _ = x_sample.shape
    mp, ms = bp * tp, bs * ts
    p = jnp.concatenate([p_prompt.reshape(DEPTH, mp, PLE_DIM), p_sample.reshape(DEPTH, ms, PLE_DIM)], axis=1)
    seq_groups = [(0, bp, tp), (mp, bs, ts)]
    y0, y1 = encoder_trunk(x_prompt.reshape(mp, d), x_sample.reshape(ms, d), p, seq_groups,
                           w_in, w_out, na_rpb, lam_q1, lam_k1, lam_q2, lam_k2,
                           diff_g, sink, rel_bias, ffn1_gate, ffn1_up, ffn1_down,
                           ffn2_gate, ffn2_up, ffn2_down, ple_gate, ple_proj, ln_g, ln_b)
    return y0.reshape(bp, tp, d), y1.reshape(bs, ts, d)
```
